```python
import jax, jax.numpy as jnp
from jax import lax
import numpy as np

D_MODEL = 2048
BATCH = 1
SEQ = 8192
DEPTH = 1
DEC_BATCH = 32
DEC_SEQ = 8
PAST_LEN = 8192
PAGE_SIZE = 128

W_M = D_MODEL // 2
H_M = 4
DV_M = W_M // H_M
DK_M = DV_M // 2
W_SB = D_MODEL - W_M
DH_SB = 128
H_SB = W_SB // DH_SB
D_MIX = W_M + W_SB
D_FF = 128 * (-(-(8 * D_MODEL) // (3 * 128)))
CONV_W = 3
MLSTM_CHUNK = 64
SB_BLOCK = 128
SB_BIAS_INIT = -8.0
LN_EPS = 1e-5
RMS_EPS = 1e-6
DN_ALPHA = (2.0 * DEPTH) ** 0.25
DN_BETA = (8.0 * DEPTH) ** -0.25
IN_SPLITS = (H_M * DK_M, H_M * DK_M, W_M, W_M, H_M, H_M, W_SB, W_SB, W_SB)
IN_SCALES = (1.0, 1.0, DN_BETA, 1.0, 1.0, 1.0, 1.0, 1.0, DN_BETA)
D_IN = sum(IN_SPLITS)

kernel_name = 'hymba_mlstm_stickbreak_convffn_step'


def layer_norm(x, g, b):
    xf = x.astype(jnp.float32)
    mu = jnp.mean(xf, axis=-1, keepdims=True)
    var = jnp.mean(jnp.square(xf - mu), axis=-1, keepdims=True)
    y = (xf - mu) * lax.rsqrt(var + LN_EPS)
    return (y * g.astype(jnp.float32) + b.astype(jnp.float32)).astype(x.dtype)


def mlstm_chunk(carry, inp):
    C, n, m = carry
    q, k, v, ig, fg = inp
    q = q.astype(jnp.float32) * (DK_M ** -0.5)
    k = k.astype(jnp.float32)
    v = v.astype(jnp.float32)
    log_f = jax.nn.log_sigmoid(fg.astype(jnp.float32))
    log_i = ig.astype(jnp.float32)
    L = q.shape[2]
    b = jnp.cumsum(log_f, axis=-1)
    a = b + m[..., None]
    causal = jnp.tril(jnp.ones((L, L), dtype=bool))
    d = jnp.where(causal, b[..., :, None] - b[..., None, :] + log_i[..., None, :], -jnp.inf)
    m_t = jnp.maximum(a, jnp.max(d, axis=-1))
    w_intra = jnp.exp(d - m_t[..., None])
    w_inter = jnp.exp(a - m_t)
    s = jnp.einsum('bhtd,bhsd->bhts', q, k) * w_intra
    num = w_inter[..., None] * jnp.einsum('bhtd,bhde->bhte', q, C) + jnp.einsum('bhts,bhse->bhte', s, v)
    den = w_inter * jnp.einsum('bhtd,bhd->bht', q, n) + jnp.sum(s, axis=-1)
    h = num / jnp.maximum(jnp.abs(den), jnp.exp(-m_t))[..., None]
    m_new = m_t[..., -1]
    w_end = jnp.exp(b[..., -1:] - b + log_i - m_new[..., None])
    decay = jnp.exp(b[..., -1] + m - m_new)
    C_new = decay[..., None, None] * C + jnp.einsum('bhs,bhsd,bhse->bhde', w_end, k, v)
    n_new = decay[..., None] * n + jnp.einsum('bhs,bhsd->bhd', w_end, k)
    return (C_new, n_new, m_new), h


def mlstm(q, k, v, ig, fg, C0, n0, m0):
    B, T = q.shape[:2]
    L = min(MLSTM_CHUNK, T)
    NC = T // L

    def chunks(x):
        x = x.reshape((B, NC, L) + x.shape[2:])
        return jnp.moveaxis(jnp.moveaxis(x, 1, 0), 3, 2)

    init = (C0.astype(jnp.float32), n0.astype(jnp.float32), m0.astype(jnp.float32))
    (C, n, m), hs = lax.scan(mlstm_chunk, init, (chunks(q), chunks(k), chunks(v), chunks(ig), chunks(fg)))
    hs = jnp.moveaxis(jnp.moveaxis(hs, 3, 2), 0, 1).reshape(B, T, H_M, DV_M)
    return hs, C, n, m


def sb_block(q_blk, q_pos, k, v, b_sb):
    z = jnp.einsum('bqhd,bkhd->bhqk', q_blk.astype(jnp.float32), k) * (DH_SB ** -0.5)
    z = z + b_sb[None, :, None, None]
    mask = jnp.arange(k.shape[1], dtype=jnp.int32)[None, :] < q_pos[:, None]
    log_1m = jnp.where(mask, jax.nn.log_sigmoid(-z), 0.0)
    log_rest = lax.cumsum(log_1m, axis=3, reverse=True) - log_1m
    a = jnp.where(mask, jnp.exp(jax.nn.log_sigmoid(z) + log_rest), 0.0)
    return jnp.einsum('bhqk,bkhd->bqhd', a, v)


def stick_breaking(q, k, v, b_sb, q_offset):
    B, T, H, D = q.shape
    QB = min(SB_BLOCK, T)
    NB = T // QB
    qb = jnp.moveaxis(q.reshape(B, NB, QB, H, D), 1, 0)
    pos = (q_offset + jnp.arange(T, dtype=jnp.int32)).reshape(NB, QB)
    kf = k.astype(jnp.float32)
    vf = v.astype(jnp.float32)
    bf = b_sb.astype(jnp.float32)
    out = lax.map(lambda qp: sb_block(qp[0], qp[1], kf, vf, bf), (qb, pos))
    return jnp.moveaxis(out, 0, 1).reshape(B, T, H, D)


def conv_ffn(x1, conv_prev, w_up, w_conv, b_conv, w_down):
    T = x1.shape[1]
    h = jnp.einsum('btd,df->btf', x1, w_up)
    h_pad = jnp.concatenate([conv_prev.astype(h.dtype), h], axis=1)
    c = b_conv + sum(w_conv[j] * h_pad[:, j:j + T] for j in range(CONV_W))
    u, g = jnp.split(c, 2, axis=-1)
    out = jnp.einsum('btf,fd->btd', jax.nn.gelu(g) * u, w_down)
    return out, h_pad[:, -(CONV_W - 1):]


def hybrid_layer(x, k_past, v_past, C0, n0, m0, conv_prev, q_offset,
                 w_in, b_gates, b_sb, w_mlstm_norm, w_out, ln1_g, ln1_b,
                 w_up, w_conv, b_conv, w_down, ln2_g, ln2_b):
    B, T, _ = x.shape
    proj = jnp.einsum('btd,de->bte', x, w_in)
    split_idx = np.cumsum(IN_SPLITS)[:-1].tolist()
    q_m, k_m, v_m, o_m, i_m, f_m, q_s, k_s, v_s = jnp.split(proj, split_idx, axis=-1)
    ig = i_m + b_gates[:H_M]
    fg = f_m + b_gates[H_M:]
    h_m, C, n, m = mlstm(q_m.reshape(B, T, H_M, DK_M), k_m.reshape(B, T, H_M, DK_M),
                         v_m.reshape(B, T, H_M, DV_M), ig, fg, C0, n0, m0)
    h_m = h_m * lax.rsqrt(jnp.mean(jnp.square(h_m), axis=-1, keepdims=True) + RMS_EPS)
    h_m = h_m * w_mlstm_norm.astype(jnp.float32).reshape(H_M, DV_M)
    h_m = jax.nn.sigmoid(o_m.astype(jnp.float32)) * h_m.reshape(B, T, W_M)
    k_new = k_s.reshape(B, T, H_SB, DH_SB)
    v_new = v_s.reshape(B, T, H_SB, DH_SB)
    k_all = k_new if k_past is None else jnp.concatenate([k_past.astype(k_new.dtype), k_new], axis=1)
    v_all = v_new if v_past is None else jnp.concatenate([v_past.astype(v_new.dtype), v_new], axis=1)
    h_s = stick_breaking(q_s.reshape(B, T, H_SB, DH_SB), k_all, v_all, b_sb, q_offset).reshape(B, T, W_SB)
    mix = jnp.einsum('btc,cd->btd', jnp.concatenate([h_m, h_s], axis=-1).astype(x.dtype), w_out)
    x1 = layer_norm(DN_ALPHA * x + mix, ln1_g, ln1_b)
    ffn, conv_state = conv_ffn(x1, conv_prev, w_up, w_conv, b_conv, w_down)
    y = layer_norm(DN_ALPHA * x1 + ffn, ln2_g, ln2_b)
    return y, k_new, v_new, C, n, m, conv_state


def setup_inputs(seed: int = 0) -> dict:
    key = jax.random.key(seed)
    ks = jax.random.split(key, 24)
    f32 = jnp.float32
    n_pages = PAST_LEN // PAGE_SIZE
    n_used = DEC_BATCH * n_pages
    n_pool = n_used + max(1, n_used // 4)
    page_table = jax.random.permutation(ks[0], n_pool)[:n_used].astype(jnp.int32).reshape(DEC_BATCH, n_pages)
    col_scale = jnp.concatenate([jnp.full((s,), c, f32) for s, c in zip(IN_SPLITS, IN_SCALES)])
    nrm = lambda k, shape: jax.random.normal(k, shape, f32)
    return {
        'x_prompt': nrm(ks[1], (BATCH, SEQ, D_MODEL)),
        'x_sample': nrm(ks[2], (DEC_BATCH, DEC_SEQ, D_MODEL)),
        'cache_k': nrm(ks[3], (DEPTH, n_pool, PAGE_SIZE, H_SB, DH_SB)),
        'cache_v': nrm(ks[4], (DEPTH, n_pool, PAGE_SIZE, H_SB, DH_SB)) * DN_BETA,
        'state_C': nrm(ks[5], (DEPTH, DEC_BATCH, H_M, DK_M, DV_M)) * 0.1,
        'state_n': nrm(ks[6], (DEPTH, DEC_BATCH, H_M, DK_M)) * 0.1,
        'state_m': nrm(ks[7], (DEPTH, DEC_BATCH, H_M)),
        'state_conv': nrm(ks[8], (DEPTH, DEC_BATCH, CONV_W - 1, 2 * D_FF)) * DN_BETA,
        'page_table': page_table,
        'w_in': nrm(ks[9], (DEPTH, D_MODEL, D_IN)) * (D_MODEL ** -0.5) * col_scale,
        'b_gates': jnp.concatenate([nrm(ks[10], (DEPTH, H_M)) * 0.1,
                                    3.0 + nrm(ks[11], (DEPTH, H_M)) * 0.5], axis=-1),
        'b_sb': SB_BIAS_INIT + 0.5 * nrm(ks[22], (DEPTH, H_SB)),
        'w_mlstm_norm': 1.0 + 0.02 * nrm(ks[12], (DEPTH, W_M)),
        'w_out': nrm(ks[13], (DEPTH, D_MIX, D_MODEL)) * (D_MIX ** -0.5) * DN_BETA,
        'ln1_g': 1.0 + 0.02 * nrm(ks[14], (DEPTH, D_MODEL)),
        'ln1_b': 0.02 * nrm(ks[15], (DEPTH, D_MODEL)),
        'w_up': nrm(ks[16], (DEPTH, D_MODEL, 2 * D_FF)) * (D_MODEL ** -0.5) * DN_BETA,
        'w_conv': nrm(ks[17], (DEPTH, CONV_W, 2 * D_FF)) * (CONV_W ** -0.5),
        'b_conv': 0.02 * nrm(ks[18], (DEPTH, 2 * D_FF)),
        'w_down': nrm(ks[19], (DEPTH, D_FF, D_MODEL)) * (D_FF ** -0.5) * DN_BETA,
        'ln2_g': 1.0 + 0.02 * nrm(ks[20], (DEPTH, D_MODEL)),
        'ln2_b': 0.02 * nrm(ks[21], (DEPTH, D_MODEL)),
    }


def reference(x_prompt, x_sample, cache_k, cache_v, state_C, state_n, state_m, state_conv,
              page_table, w_in, b_gates, b_sb, w_mlstm_norm, w_out, ln1_g, ln1_b,
              w_up, w_conv, b_conv, w_down, ln2_g, ln2_b):
    xp, xs = x_prompt, x_sample
    B = xp.shape[0]
    DB, n_pages = page_table.shape
    outs_p, outs_s = [], []
    for l in range(DEPTH):
        params = (w_in[l], b_gates[l], b_sb[l], w_mlstm_norm[l], w_out[l], ln1_g[l], ln1_b[l],
                  w_up[l], w_conv[l], b_conv[l], w_down[l], ln2_g[l], ln2_b[l])
        C0 = jnp.zeros((B, H_M, DK_M, DV_M), jnp.float32)
        n0 = jnp.zeros((B, H_M, DK_M), jnp.float32)
        m0 = jnp.zeros((B, H_M), jnp.float32)
        conv0 = jnp.zeros((B, CONV_W - 1, 2 * D_FF), xp.dtype)
        xp, *sp = hybrid_layer(xp, None, None, C0, n0, m0, conv0, 0, *params)
        outs_p.append(sp)
        k_past = cache_k[l][page_table].reshape(DB, n_pages * PAGE_SIZE, H_SB, DH_SB)
        v_past = cache_v[l][page_table].reshape(DB, n_pages * PAGE_SIZE, H_SB, DH_SB)
        xs, *ss = hybrid_layer(xs, k_past, v_past, state_C[l], state_n[l], state_m[l],
                               state_conv[l], PAST_LEN, *params)
        outs_s.append(ss)
    k_p, v_p, C_p, n_p, m_p, conv_p = [jnp.stack(t) for t in zip(*outs_p)]
    k_s, v_s, C_s, n_s, m_s, conv_s = [jnp.stack(t) for t in zip(*outs_s)]
    return (xp, xs, k_p, v_p, C_p, n_p, m_p, conv_p, k_s, v_s, C_s, n_s, m_s, conv_s)
```

```python
import functools

import jax
import jax.numpy as jnp
from jax import lax
from jax.experimental import pallas as pl
from jax.experimental.pallas import tpu as pltpu

F32 = jnp.float32
BF16 = jnp.bfloat16

D_MODEL = 2048
H_M = 4
DK_M = 128
DV_M = 256
W_M = H_M * DV_M
H_SB = 8
DH_SB = 128
W_SB = H_SB * DH_SB
D_FF = 5504
D_FF_PAD = 5632
CONV_W = 3
PAGE_SIZE = 128
LN_EPS = 1e-5
RMS_EPS = 1e-6
DN_ALPHA = 2.0 ** 0.25
GATE_COL = 3072
N_MAIN = 6144
LANES = 128
MIB = 1024 * 1024


def _params(semantics, vmem_mib):
    return pltpu.CompilerParams(dimension_semantics=semantics,
                                vmem_limit_bytes=int(vmem_mib * MIB))


def _dot(a, b):
    return jnp.dot(a, b, preferred_element_type=F32)


def _dot_nt(a, b):
    return lax.dot_general(a, b, (((1,), (1,)), ((), ())), preferred_element_type=F32)


def _dot_tn(a, b):
    return lax.dot_general(a, b, (((0,), (0,)), ((), ())), preferred_element_type=F32)


def _softplus(z):
    return jnp.maximum(z, 0.0) + jnp.log1p(jnp.exp(-jnp.abs(z)))


IN_TN = 512
IN_NJ = N_MAIN // IN_TN
IN_NJ_MAIN = 4096 // IN_TN


def _inproj_kernel(x_ref, w_ref, wg_ref, p_ref, k_ref, v_ref, g_ref, xb_ref):
    j = pl.program_id(1)

    @pl.when(j == 0)
    def _():
        xb = x_ref[...].astype(BF16)
        xb_ref[...] = xb
        g_ref[...] = _dot(xb, wg_ref[...])

    acc = _dot(xb_ref[...], w_ref[...])

    @pl.when(j < IN_NJ_MAIN)
    def _():
        p_ref[...] = acc

    @pl.when((j >= IN_NJ_MAIN) & (j < IN_NJ_MAIN + 2))
    def _():
        k_ref[...] = acc

    @pl.when(j >= IN_NJ_MAIN + 2)
    def _():
        v_ref[...] = acc


def _inproj(x, w_main, w_gate, tm):
    m = x.shape[0]
    return pl.pallas_call(
        _inproj_kernel,
        grid=(m // tm, IN_NJ),
        in_specs=[
            pl.BlockSpec((tm, D_MODEL), lambda i, j: (i, 0)),
            pl.BlockSpec((D_MODEL, IN_TN), lambda i, j: (0, j)),
            pl.BlockSpec((D_MODEL, LANES), lambda i, j: (0, 0)),
        ],
        out_specs=[
            pl.BlockSpec((tm, IN_TN), lambda i, j: (i, jnp.minimum(j, IN_NJ_MAIN - 1))),
            pl.BlockSpec((tm, IN_TN), lambda i, j: (i, jnp.clip(j - IN_NJ_MAIN, 0, 1))),
            pl.BlockSpec((tm, IN_TN), lambda i, j: (i, jnp.clip(j - IN_NJ_MAIN - 2, 0, 1))),
            pl.BlockSpec((tm, LANES), lambda i, j: (i, 0)),
        ],
        out_shape=[
            jax.ShapeDtypeStruct((m, 4096), F32),
            jax.ShapeDtypeStruct((m, W_SB), F32),
            jax.ShapeDtypeStruct((m, W_SB), F32),
            jax.ShapeDtypeStruct((m, LANES), F32),
        ],
        scratch_shapes=[pltpu.VMEM((tm, D_MODEL), BF16)],
        compiler_params=_params(("parallel", "arbitrary"), 40),
        name="inproj",
    )(x, w_main, w_gate)


def _mlstm_kernel(q_ref, k_ref, v_ref, o_ref, g_ref, bg_ref, wn_ref, c0_ref, n0_ref, m0_ref,
                  h_ref, cout_ref, nout_ref, mout_ref, c_s, n_s, m_s, *, L):
    c = pl.program_id(1)

    @pl.when(c == 0)
    def _():
        c_s[...] = c0_ref[0]
        n_s[...] = n0_ref[0]
        m_s[...] = m0_ref[0]

    g = g_ref[...] + bg_ref[...]
    row = lax.broadcasted_iota(jnp.int32, (L, L), 0)
    col = lax.broadcasted_iota(jnp.int32, (L, L), 1)
    causal = col <= row
    eye = col == row
    for h in range(H_M):
        li_col = g[:, h:h + 1]
        fg_col = g[:, H_M + h:H_M + h + 1]
        lf_col = -_softplus(-fg_col)
        li_row = jnp.sum(jnp.where(eye, li_col, 0.0), axis=0, keepdims=True)
        lf_row = jnp.sum(jnp.where(eye, lf_col, 0.0), axis=0, keepdims=True)
        b_row = jnp.sum(jnp.where(row <= col, lf_col, 0.0), axis=0, keepdims=True)
        b_col = jnp.sum(jnp.where(causal, lf_row, 0.0), axis=1, keepdims=True)
        b_last = b_col[L - 1:L, :]
        m_prev = m_s[h]
        d = jnp.where(causal, b_col - b_row + li_row, -jnp.inf)
        a_col = b_col + m_prev
        m_t = jnp.maximum(a_col, jnp.max(d, axis=1, keepdims=True))
        w_intra = jnp.exp(d - m_t)
        w_inter = jnp.exp(a_col - m_t)

        qf = q_ref[:, h * DK_M:(h + 1) * DK_M] * (DK_M ** -0.5)
        kf = k_ref[:, h * DK_M:(h + 1) * DK_M]
        qb = qf.astype(BF16)
        kb = kf.astype(BF16)
        vb = v_ref[:, h * DV_M:(h + 1) * DV_M].astype(BF16)
        c_prev = c_s[h]
        n_prev = n_s[h]

        s = _dot_nt(qb, kb) * w_intra
        num = w_inter * _dot(qb, c_prev.astype(BF16)) + _dot(s.astype(BF16), vb)
        den = (w_inter * jnp.sum(qf * n_prev, axis=1, keepdims=True)
               + jnp.sum(s, axis=1, keepdims=True))
        hh = num / jnp.maximum(jnp.abs(den), jnp.exp(-m_t))

        m_new = m_t[L - 1:L, :]
        w_end = jnp.exp(b_last - b_col + li_col - m_new)
        decay = jnp.exp(b_last + m_prev - m_new)
        kw = kf * w_end
        c_s[h] = decay * c_prev + _dot_tn(kw.astype(BF16), vb)
        n_s[h] = decay * n_prev + jnp.sum(kw, axis=0, keepdims=True)
        m_s[h] = m_new

        hn = hh * lax.rsqrt(jnp.mean(hh * hh, axis=1, keepdims=True) + RMS_EPS)
        hn = hn * wn_ref[:, h * DV_M:(h + 1) * DV_M]
        og = jax.nn.sigmoid(o_ref[:, h * DV_M:(h + 1) * DV_M])
        h_ref[:, h * DV_M:(h + 1) * DV_M] = (og * hn).astype(h_ref.dtype)

    @pl.when(c == pl.num_programs(1) - 1)
    def _():
        cout_ref[0] = c_s[...]
        nout_ref[0] = n_s[...]
        mout_ref[0] = m_s[...]


def _mlstm(p, gates, bg, wn, c0, n0, m0, *, batch, L, out_dtype):
    t_total = p.shape[0]
    nc = t_total // (batch * L)
    row = lambda b, c: b * nc + c
    return pl.pallas_call(
        functools.partial(_mlstm_kernel, L=L),
        grid=(batch, nc),
        in_specs=[
            pl.BlockSpec((L, H_M * DK_M), lambda b, c: (row(b, c), 0)),
            pl.BlockSpec((L, H_M * DK_M), lambda b, c: (row(b, c), 1)),
            pl.BlockSpec((L, W_M), lambda b, c: (row(b, c), 1)),
            pl.BlockSpec((L, W_M), lambda b, c: (row(b, c), 2)),
            pl.BlockSpec((L, LANES), lambda b, c: (row(b, c), 0)),
            pl.BlockSpec((1, LANES), lambda b, c: (0, 0)),
            pl.BlockSpec((1, W_M), lambda b, c: (0, 0)),
            pl.BlockSpec((1, H_M, DK_M, DV_M), lambda b, c: (b, 0, 0, 0)),
            pl.BlockSpec((1, H_M, 1, DK_M), lambda b, c: (b, 0, 0, 0)),
            pl.BlockSpec((1, H_M, 1, 1), lambda b, c: (b, 0, 0, 0)),
        ],
        out_specs=[
            pl.BlockSpec((L, W_M), lambda b, c: (row(b, c), 0)),
            pl.BlockSpec((1, H_M, DK_M, DV_M), lambda b, c: (b, 0, 0, 0)),
            pl.BlockSpec((1, H_M, 1, DK_M), lambda b, c: (b, 0, 0, 0)),
            pl.BlockSpec((1, H_M, 1, 1), lambda b, c: (b, 0, 0, 0)),
        ],
        out_shape=[
            jax.ShapeDtypeStruct((t_total, W_M), out_dtype),
            jax.ShapeDtypeStruct((batch, H_M, DK_M, DV_M), F32),
            jax.ShapeDtypeStruct((batch, H_M, 1, DK_M), F32),
            jax.ShapeDtypeStruct((batch, H_M, 1, 1), F32),
        ],
        scratch_shapes=[
            pltpu.VMEM((H_M, DK_M, DV_M), F32),
            pltpu.VMEM((H_M, 1, DK_M), F32),
            pltpu.VMEM((H_M, 1, 1), F32),
        ],
        compiler_params=_params(("parallel", "arbitrary"), 32),
        name="mlstm",
    )(p, p, p, p, gates, bg, wn, c0, n0, m0)


SBP_BLK = 256
SBP_CAST_ROWS = 512


def _sb_prompt_kernel(bsb_ref, q_ref, k_ref, v_ref, o_ref, kb_ref, vb_ref, *, T):
    h = pl.program_id(0)
    i = pl.program_id(1)
    blk = SBP_BLK

    @pl.when(i == 0)
    def _():
        def cast(c, carry):
            sl = pl.ds(pl.multiple_of(c * SBP_CAST_ROWS, SBP_CAST_ROWS), SBP_CAST_ROWS)
            kb_ref[sl, :] = k_ref[sl, :].astype(BF16)
            vb_ref[sl, :] = v_ref[sl, :].astype(BF16)
            return carry
        lax.fori_loop(0, T // SBP_CAST_ROWS, cast, 0)

    bias = bsb_ref[h]
    qb = (q_ref[...] * (DH_SB ** -0.5)).astype(BF16)
    row = lax.broadcasted_iota(jnp.int32, (blk, blk), 0)
    col = lax.broadcasted_iota(jnp.int32, (blk, blk), 1)
    later = (row > col).astype(BF16)

    def body(jj, carry):
        acc, rest = carry
        j = i - jj
        off = pl.multiple_of(j * blk, blk)
        kblk = kb_ref[pl.ds(off, blk), :]
        vblk = vb_ref[pl.ds(off, blk), :]
        z = _dot_nt(qb, kblk) + bias
        sp = _softplus(z)
        mask = (col - jj * blk) < row
        l1m = jnp.where(mask, -sp, 0.0)
        e = _dot(l1m.astype(BF16), later)
        a = jnp.where(mask, jnp.exp(z - sp + e + rest), 0.0)
        acc = acc + _dot(a.astype(BF16), vblk)
        rest = rest + e[:, 0:1] + l1m[:, 0:1]
        return acc, rest

    acc, _ = lax.fori_loop(0, i + 1, body,
                           (jnp.zeros((blk, DH_SB), F32), jnp.zeros((blk, 1), F32)))
    o_ref[...] = acc.astype(o_ref.dtype)


def _sb_prompt(p, k, v, b_sb):
    T = p.shape[0]
    q_col0 = 3072 // DH_SB
    return pl.pallas_call(
        functools.partial(_sb_prompt_kernel, T=T),
        grid_spec=pltpu.PrefetchScalarGridSpec(
            num_scalar_prefetch=0,
            grid=(H_SB, T // SBP_BLK),
            in_specs=[
                pl.BlockSpec(memory_space=pltpu.SMEM),
                pl.BlockSpec((SBP_BLK, DH_SB), lambda h, i: (i, q_col0 + h)),
                pl.BlockSpec((T, DH_SB), lambda h, i: (0, h)),
                pl.BlockSpec((T, DH_SB), lambda h, i: (0, h)),
            ],
            out_specs=pl.BlockSpec((SBP_BLK, DH_SB), lambda h, i: (i, h)),
            scratch_shapes=[pltpu.VMEM((T, DH_SB), BF16), pltpu.VMEM((T, DH_SB), BF16)],
        ),
        out_shape=jax.ShapeDtypeStruct((T, W_SB), BF16),
        compiler_params=_params(("parallel", "arbitrary"), 40),
        name="sb_prompt",
    )(b_sb, p, k, v)


SBS_PPS = 8
SBS_ROWS = H_SB * 8


def _sb_sample_kernel(pt_ref, bsb_ref, q_ref, kn_ref, vn_ref, *rest, n_q):
    page_refs = rest[:2 * SBS_PPS]
    o_ref, acc_ref, rest_ref = rest[2 * SBS_PPS:]
    g = pl.program_id(1)
    rows = H_SB * n_q

    qs = [(q_ref[:, h * DH_SB:(h + 1) * DH_SB] * (DH_SB ** -0.5)).astype(BF16)
          for h in range(H_SB)]
    bias = jnp.concatenate([jnp.full((n_q, PAGE_SIZE), bsb_ref[h], F32) for h in range(H_SB)],
                           axis=0)
    row = lax.broadcasted_iota(jnp.int32, (PAGE_SIZE, PAGE_SIZE), 0)
    col = lax.broadcasted_iota(jnp.int32, (PAGE_SIZE, PAGE_SIZE), 1)
    later = (row > col).astype(BF16)

    def block(ks, vs, mask):
        z = jnp.concatenate([_dot_nt(qs[h], ks[h]) for h in range(H_SB)], axis=0) + bias
        sp = _softplus(z)
        l1m = -sp if mask is None else jnp.where(mask, -sp, 0.0)
        e = _dot(l1m.astype(BF16), later)
        a = jnp.exp(z - sp + e + rest_ref[...])
        if mask is not None:
            a = jnp.where(mask, a, 0.0)
        ab = a.astype(BF16)
        for h in range(H_SB):
            acc_ref[h * n_q:(h + 1) * n_q, :] += _dot(ab[h * n_q:(h + 1) * n_q, :], vs[h])
        rest_ref[...] += e[:, 0:1] + l1m[:, 0:1]

    @pl.when(g == 0)
    def _():
        acc_ref[...] = jnp.zeros_like(acc_ref)
        rest_ref[...] = jnp.zeros_like(rest_ref)
        pad = jnp.zeros((PAGE_SIZE - n_q, DH_SB), BF16)
        ks = [jnp.concatenate([kn_ref[:, h * DH_SB:(h + 1) * DH_SB].astype(BF16), pad], axis=0)
              for h in range(H_SB)]
        vs = [jnp.concatenate([vn_ref[:, h * DH_SB:(h + 1) * DH_SB].astype(BF16), pad], axis=0)
              for h in range(H_SB)]
        r = lax.broadcasted_iota(jnp.int32, (rows, PAGE_SIZE), 0)
        cidx = lax.broadcasted_iota(jnp.int32, (rows, PAGE_SIZE), 1)
        block(ks, vs, cidx < (r % n_q))

    for c in range(SBS_PPS):
        k_ref = page_refs[c]
        v_ref = page_refs[SBS_PPS + c]
        ks = [k_ref[:, h, :].astype(BF16) for h in range(H_SB)]
        vs = [v_ref[:, h, :].astype(BF16) for h in range(H_SB)]
        block(ks, vs, None)

    @pl.when(g == pl.num_programs(1) - 1)
    def _():
        for h in range(H_SB):
            o_ref[:, h * DH_SB:(h + 1) * DH_SB] = acc_ref[h * n_q:(h + 1) * n_q, :]


def _sb_sample(p, k_new, v_new, cache_k, cache_v, page_table, b_sb, *, n_q):
    n_seq, n_pages = page_table.shape
    n_groups = n_pages // SBS_PPS

    def page_spec(c):
        return pl.BlockSpec(
            (None, None, PAGE_SIZE, H_SB, DH_SB),
            lambda b, g, pt: (0, pt[b, n_pages - 1 - (g * SBS_PPS + c)], 0, 0, 0))

    return pl.pallas_call(
        functools.partial(_sb_sample_kernel, n_q=n_q),
        grid_spec=pltpu.PrefetchScalarGridSpec(
            num_scalar_prefetch=1,
            grid=(n_seq, n_groups),
            in_specs=[
                pl.BlockSpec(memory_space=pltpu.SMEM),
                pl.BlockSpec((n_q, W_SB), lambda b, g, pt: (b, 3)),
                pl.BlockSpec((n_q, W_SB), lambda b, g, pt: (b, 0)),
                pl.BlockSpec((n_q, W_SB), lambda b, g, pt: (b, 0)),
            ] + [page_spec(c) for c in range(SBS_PPS)] * 2,
            out_specs=pl.BlockSpec((n_q, W_SB), lambda b, g, pt: (b, 0)),
            scratch_shapes=[pltpu.VMEM((H_SB * n_q, DH_SB), F32),
                            pltpu.VMEM((H_SB * n_q, 1), F32)],
        ),
        out_shape=jax.ShapeDtypeStruct((n_seq * n_q, W_SB), F32),
        compiler_params=_params(("parallel", "arbitrary"), 40),
        name="sb_sample",
    )(page_table, b_sb, p, k_new, v_new,
      *([cache_k] * SBS_PPS), *([cache_v] * SBS_PPS))


def _layer_norm(x, g, b):
    mu = jnp.mean(x, axis=-1, keepdims=True)
    xc = x - mu
    var = jnp.mean(xc * xc, axis=-1, keepdims=True)
    return xc * lax.rsqrt(var + LN_EPS) * g + b


def _outproj_kernel(hm_ref, hs_ref, x_ref, wm_ref, ws_ref, g_ref, b_ref, x1_ref, x1b_ref):
    mix = _dot(hm_ref[...].astype(BF16), wm_ref[...]) + _dot(hs_ref[...].astype(BF16), ws_ref[...])
    x1 = _layer_norm(DN_ALPHA * x_ref[...] + mix, g_ref[...], b_ref[...])
    x1_ref[...] = x1
    x1b_ref[...] = x1.astype(BF16)


def _outproj(hm, hs, x, w_out, ln_g, ln_b, tm):
    m = x.shape[0]
    return pl.pallas_call(
        _outproj_kernel,
        grid=(m // tm,),
        in_specs=[
            pl.BlockSpec((tm, W_M), lambda i: (i, 0)),
            pl.BlockSpec((tm, W_SB), lambda i: (i, 0)),
            pl.BlockSpec((tm, D_MODEL), lambda i: (i, 0)),
            pl.BlockSpec((W_M, D_MODEL), lambda i: (0, 0)),
            pl.BlockSpec((W_SB, D_MODEL), lambda i: (1, 0)),
            pl.BlockSpec((1, D_MODEL), lambda i: (0, 0)),
            pl.BlockSpec((1, D_MODEL), lambda i: (0, 0)),
        ],
        out_specs=[
            pl.BlockSpec((tm, D_MODEL), lambda i: (i, 0)),
            pl.BlockSpec((tm, D_MODEL), lambda i: (i, 0)),
        ],
        out_shape=[
            jax.ShapeDtypeStruct((m, D_MODEL), F32),
            jax.ShapeDtypeStruct((m, D_MODEL), BF16),
        ],
        compiler_params=_params(("parallel",), 48),
        name="outproj_ln",
    )(hm, hs, x, w_out, w_out, ln_g, ln_b)


UP_TN = 512
UP_NJ = D_FF_PAD // UP_TN
CARRY = 8


def _gelu_tanh(x):
    return 0.5 * x * (1.0 + jnp.tanh(0.7978845608028654 * (x + 0.044715 * (x * x * x))))


def _conv_taps(buf_ref, h, wc_ref, bc_ref, first_rows):
    tm = h.shape[0]
    buf_ref[CARRY:CARRY + tm, :] = h
    h1 = buf_ref[CARRY - 1:CARRY - 1 + tm, :]
    h2 = buf_ref[CARRY - 2:CARRY - 2 + tm, :]
    if first_rows is not None:
        seg, a1, a2 = first_rows
        r = lax.broadcasted_iota(jnp.int32, h.shape, 0) % seg
        h1 = jnp.where(r < 1, a1, h1)
        h2 = jnp.where(r < 2, a2, h2)
    return bc_ref[...] + wc_ref[0:1, :] * h2 + wc_ref[1:2, :] * h1 + wc_ref[2:3, :] * h


def _ffn_up_prompt_kernel(x_ref, wu_ref, wg_ref, wcu_ref, wcg_ref, bcu_ref, bcg_ref, pu_ref, pg_ref,
                          act_ref, cu_ref, cg_ref, bu_s, bg_s, *, tm):
    i = pl.program_id(1)

    @pl.when(i == 0)
    def _():
        bu_s[CARRY - 2:CARRY, :] = pu_ref[...]
        bg_s[CARRY - 2:CARRY, :] = pg_ref[...]

    @pl.when(i > 0)
    def _():
        bu_s[0:CARRY, :] = bu_s[tm:tm + CARRY, :]
        bg_s[0:CARRY, :] = bg_s[tm:tm + CARRY, :]

    x = x_ref[...]
    hu = _dot(x, wu_ref[...])
    hg = _dot(x, wg_ref[...])
    cu = _conv_taps(bu_s, hu, wcu_ref, bcu_ref, None)
    cg = _conv_taps(bg_s, hg, wcg_ref, bcg_ref, None)
    act_ref[...] = (_gelu_tanh(cg) * cu).astype(act_ref.dtype)
    cu_ref[...] = hu[tm - 2:tm, :]
    cg_ref[...] = hg[tm - 2:tm, :]


def _ffn_up_prompt(x1b, w_up, w_conv, b_conv, conv_prev, tm):
    m = x1b.shape[0]
    col = lambda j, i: (0, j)
    colg = lambda j, i: (0, j + UP_NJ)
    return pl.pallas_call(
        functools.partial(_ffn_up_prompt_kernel, tm=tm),
        grid=(UP_NJ, m // tm),
        in_specs=[
            pl.BlockSpec((tm, D_MODEL), lambda j, i: (i, 0)),
            pl.BlockSpec((D_MODEL, UP_TN), col),
            pl.BlockSpec((D_MODEL, UP_TN), colg),
            pl.BlockSpec((CONV_W, UP_TN), col),
            pl.BlockSpec((CONV_W, UP_TN), colg),
            pl.BlockSpec((1, UP_TN), col),
            pl.BlockSpec((1, UP_TN), colg),
            pl.BlockSpec((CONV_W - 1, UP_TN), col),
            pl.BlockSpec((CONV_W - 1, UP_TN), colg),
        ],
        out_specs=[
            pl.BlockSpec((tm, UP_TN), lambda j, i: (i, j)),
            pl.BlockSpec((CONV_W - 1, UP_TN), col),
            pl.BlockSpec((CONV_W - 1, UP_TN), col),
        ],
        out_shape=[
            jax.ShapeDtypeStruct((m, D_FF_PAD), BF16),
            jax.ShapeDtypeStruct((CONV_W - 1, D_FF_PAD), F32),
            jax.ShapeDtypeStruct((CONV_W - 1, D_FF_PAD), F32),
        ],
        scratch_shapes=[pltpu.VMEM((tm + CARRY, UP_TN), F32), pltpu.VMEM((tm + CARRY, UP_TN), F32)],
        compiler_params=_params(("parallel", "arbitrary"), 40),
        name="ffn_up_prompt",
    )(x1b, w_up, w_up, w_conv, w_conv, b_conv, b_conv, conv_prev, conv_prev)


def _ffn_up_sample_kernel(x_ref, wu_ref, wg_ref, wcu_ref, wcg_ref, bcu_ref, bcg_ref,
                          a1u_ref, a2u_ref, a1g_ref, a2g_ref,
                          act_ref, hu_ref, hg_ref, bu_s, bg_s, *, seg):
    x = x_ref[...]
    hu = _dot(x, wu_ref[...])
    hg = _dot(x, wg_ref[...])
    bu_s[0:CARRY, :] = jnp.zeros((CARRY, UP_TN), F32)
    bg_s[0:CARRY, :] = jnp.zeros((CARRY, UP_TN), F32)
    cu = _conv_taps(bu_s, hu, wcu_ref, bcu_ref, (seg, a1u_ref[...], a2u_ref[...]))
    cg = _conv_taps(bg_s, hg, wcg_ref, bcg_ref, (seg, a1g_ref[...], a2g_ref[...]))
    act_ref[...] = (_gelu_tanh(cg) * cu).astype(act_ref.dtype)
    hu_ref[...] = hu
    hg_ref[...] = hg


def _ffn_up_sample(x1b, w_up, w_conv, b_conv, a1, a2, seg):
    m = x1b.shape[0]
    col = lambda j: (0, j)
    colg = lambda j: (0, j + UP_NJ)
    return pl.pallas_call(
        functools.partial(_ffn_up_sample_kernel, seg=seg),
        grid=(UP_NJ,),
        in_specs=[
            pl.BlockSpec((m, D_MODEL), lambda j: (0, 0)),
            pl.BlockSpec((D_MODEL, UP_TN), col),
            pl.BlockSpec((D_MODEL, UP_TN), colg),
            pl.BlockSpec((CONV_W, UP_TN), col),
            pl.BlockSpec((CONV_W, UP_TN), colg),
            pl.BlockSpec((1, UP_TN), col),
            pl.BlockSpec((1, UP_TN), colg),
            pl.BlockSpec((m, UP_TN), col),
            pl.BlockSpec((m, UP_TN), col),
            pl.BlockSpec((m, UP_TN), colg),
            pl.BlockSpec((m, UP_TN), colg),
        ],
        out_specs=[
            pl.BlockSpec((m, UP_TN), col),
            pl.BlockSpec((m, UP_TN), col),
            pl.BlockSpec((m, UP_TN), col),
        ],
        out_shape=[
            jax.ShapeDtypeStruct((m, D_FF_PAD), BF16),
            jax.ShapeDtypeStruct((m, D_FF_PAD), F32),
            jax.ShapeDtypeStruct((m, D_FF_PAD), F32),
        ],
        scratch_shapes=[pltpu.VMEM((m + CARRY, UP_TN), F32), pltpu.VMEM((m + CARRY, UP_TN), F32)],
        compiler_params=_params(("parallel",), 32),
        name="ffn_up_sample",
    )(x1b, w_up, w_up, w_conv, w_conv, b_conv, b_conv, a1, a2, a1, a2)


DOWN_TK = 512


def _ffn_down_kernel(a_ref, w_ref, x1_ref, g_ref, b_ref, y_ref, acc_ref):
    k = pl.program_id(1)

    @pl.when(k == 0)
    def _():
        acc_ref[...] = jnp.zeros_like(acc_ref)

    acc_ref[...] += _dot(a_ref[...], w_ref[...])

    @pl.when(k == pl.num_programs(1) - 1)
    def _():
        y_ref[...] = _layer_norm(DN_ALPHA * x1_ref[...] + acc_ref[...], g_ref[...], b_ref[...])


def _ffn_down(act, w_down, x1, ln_g, ln_b, tm):
    m = act.shape[0]
    return pl.pallas_call(
        _ffn_down_kernel,
        grid=(m // tm, D_FF_PAD // DOWN_TK),
        in_specs=[
            pl.BlockSpec((tm, DOWN_TK), lambda i, k: (i, k)),
            pl.BlockSpec((DOWN_TK, D_MODEL), lambda i, k: (k, 0)),
            pl.BlockSpec((tm, D_MODEL), lambda i, k: (i, 0)),
            pl.BlockSpec((1, D_MODEL), lambda i, k: (0, 0)),
            pl.BlockSpec((1, D_MODEL), lambda i, k: (0, 0)),
        ],
        out_specs=pl.BlockSpec((tm, D_MODEL), lambda i, k: (i, 0)),
        out_shape=jax.ShapeDtypeStruct((m, D_MODEL), F32),
        scratch_shapes=[pltpu.VMEM((tm, D_MODEL), F32)],
        compiler_params=_params(("parallel", "arbitrary"), 40),
        name="ffn_down_ln",
    )(act, w_down, x1, ln_g, ln_b)


def _pad_cols(a, width):
    return jnp.pad(a, ((0, 0), (0, width - a.shape[1])))


def _split_pad(a):
    return jnp.concatenate([_pad_cols(a[:, :D_FF], D_FF_PAD), _pad_cols(a[:, D_FF:], D_FF_PAD)], axis=1)


def _unsplit(u, g):
    return jnp.concatenate([u[..., :D_FF], g[..., :D_FF]], axis=-1)


def kernel(x_prompt, x_sample, cache_k, cache_v, state_C, state_n, state_m, state_conv, page_table,
           w_in, b_gates, b_sb, w_mlstm_norm, w_out, ln1_g, ln1_b, w_up, w_conv, b_conv, w_down,
           ln2_g, ln2_b):
    n_b, seq, _ = x_prompt.shape
    n_s, n_q, _ = x_sample.shape
    assert n_b == 1

    wi = w_in[0]
    w_main = jnp.concatenate([wi[:, :GATE_COL], wi[:, GATE_COL + 2 * H_M:]], axis=1).astype(BF16)
    w_gate = _pad_cols(wi[:, GATE_COL:GATE_COL + 2 * H_M], LANES).astype(BF16)
    bg = _pad_cols(b_gates[0][None, :], LANES)
    wn = w_mlstm_norm[0][None, :]
    wo = w_out[0].astype(BF16)
    wu = _split_pad(w_up[0]).astype(BF16)
    wc = _split_pad(w_conv[0])
    bc = _split_pad(b_conv[0][None, :])
    wd = jnp.pad(w_down[0], ((0, D_FF_PAD - D_FF), (0, 0))).astype(BF16)
    g1, b1 = ln1_g[0][None, :], ln1_b[0][None, :]
    g2, b2 = ln2_g[0][None, :], ln2_b[0][None, :]
    bsb = b_sb[0]

    xp = x_prompt.reshape(seq, D_MODEL)
    p_p, k_p, v_p, gt_p = _inproj(xp, w_main, w_gate, 512)
    hm_p, c_p, nn_p, mm_p = _mlstm(
        p_p, gt_p, bg, wn,
        jnp.zeros((1, H_M, DK_M, DV_M), F32), jnp.zeros((1, H_M, 1, DK_M), F32),
        jnp.zeros((1, H_M, 1, 1), F32), batch=1, L=256, out_dtype=BF16)
    hs_p = _sb_prompt(p_p, k_p, v_p, bsb)
    x1_p, x1b_p = _outproj(hm_p, hs_p, xp, wo, g1, b1, 256)
    act_p, cu_p, cg_p = _ffn_up_prompt(x1b_p, wu, wc, bc, jnp.zeros((CONV_W - 1, 2 * D_FF_PAD), F32), 512)
    y_p = _ffn_down(act_p, wd, x1_p, g2, b2, 512)

    xs = x_sample.reshape(n_s * n_q, D_MODEL)
    p_s, k_s, v_s, gt_s = _inproj(xs, w_main, w_gate, n_s * n_q)
    hm_s, c_s, nn_s, mm_s = _mlstm(
        p_s, gt_s, bg, wn, state_C[0], state_n[0][:, :, None, :], state_m[0][:, :, None, None],
        batch=n_s, L=n_q, out_dtype=F32)
    hs_s = _sb_sample(p_s, k_s, v_s, cache_k, cache_v, page_table, bsb, n_q=n_q)
    x1_s, x1b_s = _outproj(hm_s, hs_s, xs, wo, g1, b1, n_s * n_q)
    st = _split_pad(state_conv[0].reshape(n_s * (CONV_W - 1), 2 * D_FF)).reshape(n_s, CONV_W - 1, 2 * D_FF_PAD)
    zeros = jnp.zeros((n_s, n_q, 2 * D_FF_PAD), F32)
    a1 = zeros.at[:, 0].set(st[:, 1]).reshape(n_s * n_q, 2 * D_FF_PAD)
    a2 = zeros.at[:, 0].set(st[:, 0]).at[:, 1].set(st[:, 1]).reshape(n_s * n_q, 2 * D_FF_PAD)
    act_s, hu_s, hg_s = _ffn_up_sample(x1b_s, wu, wc, bc, a1, a2, n_q)
    y_s = _ffn_down(act_s, wd, x1_s, g2, b2, n_s * n_q)

    conv_s = _unsplit(hu_s.reshape(n_s, n_q, D_FF_PAD)[:, n_q - 2:], hg_s.reshape(n_s, n_q, D_FF_PAD)[:, n_q - 2:])
    return (
        y_p.reshape(1, seq, D_MODEL),
        y_s.reshape(n_s, n_q, D_MODEL),
        k_p.reshape(1, 1, seq, H_SB, DH_SB),
        v_p.reshape(1, 1, seq, H_SB, DH_SB),
        c_p[None],
        nn_p.reshape(1, 1, H_M, DK_M),
        mm_p.reshape(1, 1, H_M),
        _unsplit(cu_p, cg_p)[None, None],
        k_s.reshape(1, n_s, n_q, H_SB, DH_SB),
        v_s.reshape(1, n_s, n_q, H_SB, DH_SB),
        c_s[None],
        nn_s.reshape(1, n_s, H_M, DK_M),
        mm_s.reshape(1, n_s, H_M),
        conv_s[None],
    )
```

```python
import functools

import jax
import jax.numpy as jnp
from jax import lax
from jax.experimental import pallas as pl
from jax.experimental.pallas import tpu as pltpu

F32 = jnp.float32
BF16 = jnp.bfloat16

D_MODEL = 2048
H_M = 4
DK_M = 128
DV_M = 256
W_M = H_M * DV_M
H_SB = 8
DH_SB = 128
W_SB = H_SB * DH_SB
D_FF = 5504
D_FF_PAD = 5632
CONV_W = 3
PAGE_SIZE = 128
LN_EPS = 1e-5
RMS_EPS = 1e-6
DN_ALPHA = 2.0 ** 0.25
GATE_COL = 3072
N_MAIN = 6144
LANES = 128
MIB = 1024 * 1024


def _params(semantics, vmem_mib):
    return pltpu.CompilerParams(dimension_semantics=semantics,
                                vmem_limit_bytes=int(vmem_mib * MIB))


def _dot(a, b):
    return jnp.dot(a, b, preferred_element_type=F32)


def _dot_nt(a, b):
    return lax.dot_general(a, b, (((1,), (1,)), ((), ())), preferred_element_type=F32)


def _dot_tn(a, b):
    return lax.dot_general(a, b, (((0,), (0,)), ((), ())), preferred_element_type=F32)


def _softplus(z):
    return jnp.maximum(z, 0.0) + jnp.log1p(jnp.exp(-jnp.abs(z)))


IN_TN = 512
IN_NJ = N_MAIN // IN_TN
IN_NJ_MAIN = 4096 // IN_TN


def _inproj_kernel(x_ref, w_ref, wg_ref, p_ref, k_ref, v_ref, kb_ref, vb_ref, g_ref, xb_ref):
    j = pl.program_id(1)

    @pl.when(j == 0)
    def _():
        xb = x_ref[...].astype(BF16)
        xb_ref[...] = xb
        g_ref[...] = _dot(xb, wg_ref[...])

    acc = _dot(xb_ref[...], w_ref[...])

    @pl.when(j < IN_NJ_MAIN)
    def _():
        p_ref[...] = acc

    @pl.when((j >= IN_NJ_MAIN) & (j < IN_NJ_MAIN + 2))
    def _():
        k_ref[...] = acc
        kb_ref[...] = acc.astype(BF16)

    @pl.when(j >= IN_NJ_MAIN + 2)
    def _():
        v_ref[...] = acc
        vb_ref[...] = acc.astype(BF16)


def _inproj(x, w_main, w_gate, tm):
    m = x.shape[0]
    return pl.pallas_call(
        _inproj_kernel,
        grid=(m // tm, IN_NJ),
        in_specs=[
            pl.BlockSpec((tm, D_MODEL), lambda i, j: (i, 0)),
            pl.BlockSpec((D_MODEL, IN_TN), lambda i, j: (0, j)),
            pl.BlockSpec((D_MODEL, LANES), lambda i, j: (0, 0)),
        ],
        out_specs=[
            pl.BlockSpec((tm, IN_TN), lambda i, j: (i, jnp.minimum(j, IN_NJ_MAIN - 1))),
            pl.BlockSpec((tm, IN_TN), lambda i, j: (i, jnp.clip(j - IN_NJ_MAIN, 0, 1))),
            pl.BlockSpec((tm, IN_TN), lambda i, j: (i, jnp.clip(j - IN_NJ_MAIN - 2, 0, 1))),
            pl.BlockSpec((tm, IN_TN), lambda i, j: (i, jnp.clip(j - IN_NJ_MAIN, 0, 1))),
            pl.BlockSpec((tm, IN_TN), lambda i, j: (i, jnp.clip(j - IN_NJ_MAIN - 2, 0, 1))),
            pl.BlockSpec((tm, LANES), lambda i, j: (i, 0)),
        ],
        out_shape=[
            jax.ShapeDtypeStruct((m, 4096), F32),
            jax.ShapeDtypeStruct((m, W_SB), F32),
            jax.ShapeDtypeStruct((m, W_SB), F32),
            jax.ShapeDtypeStruct((m, W_SB), BF16),
            jax.ShapeDtypeStruct((m, W_SB), BF16),
            jax.ShapeDtypeStruct((m, LANES), F32),
        ],
        scratch_shapes=[pltpu.VMEM((tm, D_MODEL), BF16)],
        compiler_params=_params(("parallel", "arbitrary"), 40),
        name="inproj",
    )(x, w_main, w_gate)


def _mlstm_kernel(q_ref, k_ref, v_ref, o_ref, g_ref, bg_ref, wn_ref, c0_ref, n0_ref, m0_ref,
                  h_ref, cout_ref, nout_ref, mout_ref, c_s, n_s, m_s, *, L):
    c = pl.program_id(1)

    @pl.when(c == 0)
    def _():
        c_s[...] = c0_ref[0]
        n_s[...] = n0_ref[0]
        m_s[...] = m0_ref[0]

    g = g_ref[...] + bg_ref[...]
    row = lax.broadcasted_iota(jnp.int32, (L, L), 0)
    col = lax.broadcasted_iota(jnp.int32, (L, L), 1)
    causal = col <= row
    eye = col == row
    for h in range(H_M):
        li_col = g[:, h:h + 1]
        fg_col = g[:, H_M + h:H_M + h + 1]
        lf_col = -_softplus(-fg_col)
        li_row = jnp.sum(jnp.where(eye, li_col, 0.0), axis=0, keepdims=True)
        lf_row = jnp.sum(jnp.where(eye, lf_col, 0.0), axis=0, keepdims=True)
        b_row = jnp.sum(jnp.where(row <= col, lf_col, 0.0), axis=0, keepdims=True)
        b_col = jnp.sum(jnp.where(causal, lf_row, 0.0), axis=1, keepdims=True)
        b_last = b_col[L - 1:L, :]
        m_prev = m_s[h]
        d = jnp.where(causal, b_col - b_row + li_row, -jnp.inf)
        a_col = b_col + m_prev
        m_t = jnp.maximum(a_col, jnp.max(d, axis=1, keepdims=True))
        w_intra = jnp.exp(d - m_t)
        w_inter = jnp.exp(a_col - m_t)

        qf = q_ref[:, h * DK_M:(h + 1) * DK_M] * (DK_M ** -0.5)
        kf = k_ref[:, h * DK_M:(h + 1) * DK_M]
        qb = qf.astype(BF16)
        kb = kf.astype(BF16)
        vb = v_ref[:, h * DV_M:(h + 1) * DV_M].astype(BF16)
        c_prev = c_s[h]
        n_prev = n_s[h]

        s = _dot_nt(qb, kb) * w_intra
        num = w_inter * _dot(qb, c_prev.astype(BF16)) + _dot(s.astype(BF16), vb)
        den = (w_inter * jnp.sum(qf * n_prev, axis=1, keepdims=True)
               + jnp.sum(s, axis=1, keepdims=True))
        hh = num / jnp.maximum(jnp.abs(den), jnp.exp(-m_t))

        m_new = m_t[L - 1:L, :]
        w_end = jnp.exp(b_last - b_col + li_col - m_new)
        decay = jnp.exp(b_last + m_prev - m_new)
        kw = kf * w_end
        c_s[h] = decay * c_prev + _dot_tn(kw.astype(BF16), vb)
        n_s[h] = decay * n_prev + jnp.sum(kw, axis=0, keepdims=True)
        m_s[h] = m_new

        hn = hh * lax.rsqrt(jnp.mean(hh * hh, axis=1, keepdims=True) + RMS_EPS)
        hn = hn * wn_ref[:, h * DV_M:(h + 1) * DV_M]
        og = jax.nn.sigmoid(o_ref[:, h * DV_M:(h + 1) * DV_M])
        h_ref[:, h * DV_M:(h + 1) * DV_M] = (og * hn).astype(h_ref.dtype)

    @pl.when(c == pl.num_programs(1) - 1)
    def _():
        cout_ref[0] = c_s[...]
        nout_ref[0] = n_s[...]
        mout_ref[0] = m_s[...]


def _mlstm(p, gates, bg, wn, c0, n0, m0, *, batch, L, out_dtype):
    t_total = p.shape[0]
    nc = t_total // (batch * L)
    row = lambda b, c: b * nc + c
    return pl.pallas_call(
        functools.partial(_mlstm_kernel, L=L),
        grid=(batch, nc),
        in_specs=[
            pl.BlockSpec((L, H_M * DK_M), lambda b, c: (row(b, c), 0)),
            pl.BlockSpec((L, H_M * DK_M), lambda b, c: (row(b, c), 1)),
            pl.BlockSpec((L, W_M), lambda b, c: (row(b, c), 1)),
            pl.BlockSpec((L, W_M), lambda b, c: (row(b, c), 2)),
            pl.BlockSpec((L, LANES), lambda b, c: (row(b, c), 0)),
            pl.BlockSpec((1, LANES), lambda b, c: (0, 0)),
            pl.BlockSpec((1, W_M), lambda b, c: (0, 0)),
            pl.BlockSpec((1, H_M, DK_M, DV_M), lambda b, c: (b, 0, 0, 0)),
            pl.BlockSpec((1, H_M, 1, DK_M), lambda b, c: (b, 0, 0, 0)),
            pl.BlockSpec((1, H_M, 1, 1), lambda b, c: (b, 0, 0, 0)),
        ],
        out_specs=[
            pl.BlockSpec((L, W_M), lambda b, c: (row(b, c), 0)),
            pl.BlockSpec((1, H_M, DK_M, DV_M), lambda b, c: (b, 0, 0, 0)),
            pl.BlockSpec((1, H_M, 1, DK_M), lambda b, c: (b, 0, 0, 0)),
            pl.BlockSpec((1, H_M, 1, 1), lambda b, c: (b, 0, 0, 0)),
        ],
        out_shape=[
            jax.ShapeDtypeStruct((t_total, W_M), out_dtype),
            jax.ShapeDtypeStruct((batch, H_M, DK_M, DV_M), F32),
            jax.ShapeDtypeStruct((batch, H_M, 1, DK_M), F32),
            jax.ShapeDtypeStruct((batch, H_M, 1, 1), F32),
        ],
        scratch_shapes=[
            pltpu.VMEM((H_M, DK_M, DV_M), F32),
            pltpu.VMEM((H_M, 1, DK_M), F32),
            pltpu.VMEM((H_M, 1, 1), F32),
        ],
        compiler_params=_params(("parallel", "arbitrary"), 32),
        name="mlstm",
    )(p, p, p, p, gates, bg, wn, c0, n0, m0)


SBP_BQ = 512
SBP_BK = 256
SBP_NH = 2
LOG2E = 1.4426950408889634
SB_QSCALE = (DH_SB ** -0.5) * LOG2E


def _later_aug(n):
    row = lax.broadcasted_iota(jnp.int32, (n, n + LANES), 0)
    col = lax.broadcasted_iota(jnp.int32, (n, n + LANES), 1)
    return ((row > col) | (col >= n)).astype(BF16)


def _sb_tile(z, later, v, rest, mask):
    sp = jnp.maximum(z, 0.0) + jnp.log2(1.0 + jnp.exp2(-jnp.abs(z)))
    spm = sp if mask is None else jnp.where(mask, sp, 0.0)
    e = _dot(spm.astype(BF16), later)
    a = jnp.exp2(z - sp - e + rest)
    if mask is not None:
        a = jnp.where(mask, a, 0.0)
    return _dot(a.astype(BF16), v), jnp.sum(spm, axis=1, keepdims=True)


def _sb_prompt_kernel(bsb_ref, q_ref, k_ref, v_ref, o_ref, acc_ref, rest_ref):
    hp = pl.program_id(0)
    i = pl.program_id(1)
    bq, bk = SBP_BQ, SBP_BK
    qs = [(q_ref[:, hh * DH_SB:(hh + 1) * DH_SB] * SB_QSCALE).astype(BF16) for hh in range(SBP_NH)]
    bias = [bsb_ref[hp * SBP_NH + hh] * LOG2E for hh in range(SBP_NH)]
    krow = lax.broadcasted_iota(jnp.int32, (bk, bk), 0)
    kcol = lax.broadcasted_iota(jnp.int32, (bk, bk), 1)
    later = (krow > kcol).astype(BF16)
    row = lax.broadcasted_iota(jnp.int32, (bq, bk), 0)
    col = lax.broadcasted_iota(jnp.int32, (bq, bk), 1)
    acc_ref[...] = jnp.zeros_like(acc_ref)
    rest_ref[...] = jnp.zeros_like(rest_ref)

    def step(jj, masked):
        j = (bq // bk) * (i + 1) - 1 - jj
        off = pl.multiple_of(j * bk, bk)
        mask = (col + bk * (bq // bk - 1 - jj)) < row if masked else None
        for hh in range(SBP_NH):
            kblk = k_ref[pl.ds(off, bk), hh * DH_SB:(hh + 1) * DH_SB]
            vblk = v_ref[pl.ds(off, bk), hh * DH_SB:(hh + 1) * DH_SB]
            z = _dot_nt(qs[hh], kblk) + bias[hh]
            out, tot = _sb_tile(z, later, vblk, rest_ref[hh], mask)
            acc_ref[hh] += out
            rest_ref[hh] -= tot

    def masked_body(jj, carry):
        step(jj, True)
        return carry

    def plain_body(jj, carry):
        step(jj, False)
        return carry

    lax.fori_loop(0, bq // bk, masked_body, 0)
    lax.fori_loop(bq // bk, (bq // bk) * (i + 1), plain_body, 0)
    for hh in range(SBP_NH):
        o_ref[:, hh * DH_SB:(hh + 1) * DH_SB] = acc_ref[hh].astype(o_ref.dtype)


def _sb_prompt(p, kb, vb, b_sb):
    T = p.shape[0]
    width = SBP_NH * DH_SB
    q_col0 = 3072 // width
    return pl.pallas_call(
        _sb_prompt_kernel,
        grid=(H_SB // SBP_NH, T // SBP_BQ),
        in_specs=[
            pl.BlockSpec(memory_space=pltpu.SMEM),
            pl.BlockSpec((SBP_BQ, width), lambda h, i: (i, q_col0 + h)),
            pl.BlockSpec((T, width), lambda h, i: (0, h)),
            pl.BlockSpec((T, width), lambda h, i: (0, h)),
        ],
        out_specs=pl.BlockSpec((SBP_BQ, width), lambda h, i: (i, h)),
        out_shape=jax.ShapeDtypeStruct((T, W_SB), BF16),
        scratch_shapes=[pltpu.VMEM((SBP_NH, SBP_BQ, DH_SB), F32),
                        pltpu.VMEM((SBP_NH, SBP_BQ, 1), F32)],
        compiler_params=_params(("parallel", "arbitrary"), 40),
        name="sb_prompt",
    )(b_sb, p, kb, vb)


SBS_PPS = 8
SBS_ROWS = H_SB * 8


def _sb_sample_kernel(pt_ref, bsb_ref, q_ref, kn_ref, vn_ref, *rest, n_q):
    page_refs = rest[:2 * SBS_PPS]
    o_ref, acc_ref, rest_ref = rest[2 * SBS_PPS:]
    g = pl.program_id(1)
    rows = H_SB * n_q

    qs = [(q_ref[:, h * DH_SB:(h + 1) * DH_SB] * SB_QSCALE).astype(BF16) for h in range(H_SB)]
    bias = jnp.concatenate([jnp.full((n_q, PAGE_SIZE), bsb_ref[h] * LOG2E, F32) for h in range(H_SB)],
                           axis=0)
    later_aug = _later_aug(PAGE_SIZE)

    def block(ks, vs, mask, acc, rest_v):
        z = jnp.concatenate([_dot_nt(qs[h], ks[h]) for h in range(H_SB)], axis=0) + bias
        sp = jnp.maximum(z, 0.0) + jnp.log2(1.0 + jnp.exp2(-jnp.abs(z)))
        spm = sp if mask is None else jnp.where(mask, sp, 0.0)
        ea = _dot(spm.astype(BF16), later_aug)
        a = jnp.exp2(z - sp - ea[:, :PAGE_SIZE] + rest_v)
        if mask is not None:
            a = jnp.where(mask, a, 0.0)
        ab = a.astype(BF16)
        out = jnp.concatenate([_dot(ab[h * n_q:(h + 1) * n_q, :], vs[h]) for h in range(H_SB)], axis=0)
        return acc + out, rest_v - ea[:, PAGE_SIZE:]

    @pl.when(g == 0)
    def _():
        pad = jnp.zeros((PAGE_SIZE - n_q, DH_SB), BF16)
        ks = [jnp.concatenate([kn_ref[:, h * DH_SB:(h + 1) * DH_SB].astype(BF16), pad], axis=0)
              for h in range(H_SB)]
        vs = [jnp.concatenate([vn_ref[:, h * DH_SB:(h + 1) * DH_SB].astype(BF16), pad], axis=0)
              for h in range(H_SB)]
        r = lax.broadcasted_iota(jnp.int32, (rows, PAGE_SIZE), 0)
        cidx = lax.broadcasted_iota(jnp.int32, (rows, PAGE_SIZE), 1)
        acc0, rest0 = block(ks, vs, cidx < (r % n_q), jnp.zeros((rows, DH_SB), F32),
                            jnp.zeros((rows, LANES), F32))
        acc_ref[...] = acc0
        rest_ref[...] = rest0

    def head_rows(refs, h):
        return jnp.concatenate([r[pl.ds(h, PAGE_SIZE, stride=H_SB), :] for r in refs],
                               axis=0).astype(BF16)

    ks = [head_rows(page_refs[:SBS_PPS], h) for h in range(H_SB)]
    vs = [head_rows(page_refs[SBS_PPS:], h) for h in range(H_SB)]
    z = jnp.concatenate([_dot_nt(qs[h], ks[h]) for h in range(H_SB)], axis=0)
    z = z + jnp.concatenate([bias] * SBS_PPS, axis=1)
    sp = jnp.maximum(z, 0.0) + jnp.log2(1.0 + jnp.exp2(-jnp.abs(z)))
    spb = sp.astype(BF16)
    rest_v = rest_ref[...]
    es, rests = [], []
    for c in range(SBS_PPS):
        ea = _dot(spb[:, c * PAGE_SIZE:(c + 1) * PAGE_SIZE], later_aug)
        es.append(ea[:, :PAGE_SIZE])
        rests.append(rest_v)
        rest_v = rest_v - ea[:, PAGE_SIZE:]
    a = jnp.exp2(z - sp - jnp.concatenate(es, axis=1) + jnp.concatenate(rests, axis=1))
    ab = a.astype(BF16)
    out = jnp.concatenate([_dot(ab[h * n_q:(h + 1) * n_q, :], vs[h]) for h in range(H_SB)], axis=0)
    acc = acc_ref[...] + out
    acc_ref[...] = acc
    rest_ref[...] = rest_v

    @pl.when(g == pl.num_programs(1) - 1)
    def _():
        for h in range(H_SB):
            o_ref[:, h * DH_SB:(h + 1) * DH_SB] = acc[h * n_q:(h + 1) * n_q, :]


def _sb_sample(p, k_new, v_new, cache_k, cache_v, page_table, b_sb, *, n_q):
    n_seq, n_pages = page_table.shape
    n_groups = n_pages // SBS_PPS

    def page_spec(c):
        return pl.BlockSpec(
            (PAGE_SIZE * H_SB, DH_SB),
            lambda b, g, pt: (pt[b, n_pages - 1 - (g * SBS_PPS + c)], 0))

    return pl.pallas_call(
        functools.partial(_sb_sample_kernel, n_q=n_q),
        grid_spec=pltpu.PrefetchScalarGridSpec(
            num_scalar_prefetch=1,
            grid=(n_seq, n_groups),
            in_specs=[
                pl.BlockSpec(memory_space=pltpu.SMEM),
                pl.BlockSpec((n_q, W_SB), lambda b, g, pt: (b, 3)),
                pl.BlockSpec((n_q, W_SB), lambda b, g, pt: (b, 0)),
                pl.BlockSpec((n_q, W_SB), lambda b, g, pt: (b, 0)),
            ] + [page_spec(c) for c in range(SBS_PPS)] * 2,
            out_specs=pl.BlockSpec((n_q, W_SB), lambda b, g, pt: (b, 0)),
            scratch_shapes=[pltpu.VMEM((H_SB * n_q, DH_SB), F32),
                            pltpu.VMEM((H_SB * n_q, LANES), F32)],
        ),
        out_shape=jax.ShapeDtypeStruct((n_seq * n_q, W_SB), F32),
        compiler_params=_params(("parallel", "arbitrary"), 40),
        name="sb_sample",
    )(page_table, b_sb, p, k_new, v_new,
      *([cache_k] * SBS_PPS), *([cache_v] * SBS_PPS))


def _layer_norm(x, g, b):
    mu = jnp.mean(x, axis=-1, keepdims=True)
    xc = x - mu
    var = jnp.mean(xc * xc, axis=-1, keepdims=True)
    return xc * lax.rsqrt(var + LN_EPS) * g + b


def _outproj_kernel(hm_ref, hs_ref, x_ref, wm_ref, ws_ref, g_ref, b_ref, x1_ref, x1b_ref):
    mix = _dot(hm_ref[...].astype(BF16), wm_ref[...]) + _dot(hs_ref[...].astype(BF16), ws_ref[...])
    x1 = _layer_norm(DN_ALPHA * x_ref[...] + mix, g_ref[...], b_ref[...])
    x1_ref[...] = x1
    x1b_ref[...] = x1.astype(BF16)


def _outproj(hm, hs, x, w_out, ln_g, ln_b, tm):
    m = x.shape[0]
    return pl.pallas_call(
        _outproj_kernel,
        grid=(m // tm,),
        in_specs=[
            pl.BlockSpec((tm, W_M), lambda i: (i, 0)),
            pl.BlockSpec((tm, W_SB), lambda i: (i, 0)),
            pl.BlockSpec((tm, D_MODEL), lambda i: (i, 0)),
            pl.BlockSpec((W_M, D_MODEL), lambda i: (0, 0)),
            pl.BlockSpec((W_SB, D_MODEL), lambda i: (1, 0)),
            pl.BlockSpec((1, D_MODEL), lambda i: (0, 0)),
            pl.BlockSpec((1, D_MODEL), lambda i: (0, 0)),
        ],
        out_specs=[
            pl.BlockSpec((tm, D_MODEL), lambda i: (i, 0)),
            pl.BlockSpec((tm, D_MODEL), lambda i: (i, 0)),
        ],
        out_shape=[
            jax.ShapeDtypeStruct((m, D_MODEL), F32),
            jax.ShapeDtypeStruct((m, D_MODEL), BF16),
        ],
        compiler_params=_params(("parallel",), 48),
        name="outproj_ln",
    )(hm, hs, x, w_out, w_out, ln_g, ln_b)


UP_TN = 512
UP_NJ = D_FF_PAD // UP_TN
CARRY = 8


def _gelu_tanh(x):
    return 0.5 * x * (1.0 + jnp.tanh(0.7978845608028654 * (x + 0.044715 * (x * x * x))))


def _conv_taps(buf_ref, h, wc_ref, bc_ref, first_rows):
    tm = h.shape[0]
    buf_ref[CARRY:CARRY + tm, :] = h
    h1 = buf_ref[CARRY - 1:CARRY - 1 + tm, :]
    h2 = buf_ref[CARRY - 2:CARRY - 2 + tm, :]
    if first_rows is not None:
        seg, a1, a2 = first_rows
        r = lax.broadcasted_iota(jnp.int32, h.shape, 0) % seg
        h1 = jnp.where(r < 1, a1, h1)
        h2 = jnp.where(r < 2, a2, h2)
    return bc_ref[...] + wc_ref[0:1, :] * h2 + wc_ref[1:2, :] * h1 + wc_ref[2:3, :] * h


def _ffn_up_prompt_kernel(x_ref, wu_ref, wg_ref, wcu_ref, wcg_ref, bcu_ref, bcg_ref, pu_ref, pg_ref,
                          act_ref, cu_ref, cg_ref, bu_s, bg_s, *, tm):
    i = pl.program_id(1)

    @pl.when(i == 0)
    def _():
        bu_s[CARRY - 2:CARRY, :] = pu_ref[...]
        bg_s[CARRY - 2:CARRY, :] = pg_ref[...]

    @pl.when(i > 0)
    def _():
        bu_s[0:CARRY, :] = bu_s[tm:tm + CARRY, :]
        bg_s[0:CARRY, :] = bg_s[tm:tm + CARRY, :]

    x = x_ref[...]
    hu = _dot(x, wu_ref[...])
    hg = _dot(x, wg_ref[...])
    cu = _conv_taps(bu_s, hu, wcu_ref, bcu_ref, None)
    cg = _conv_taps(bg_s, hg, wcg_ref, bcg_ref, None)
    act_ref[...] = (_gelu_tanh(cg) * cu).astype(act_ref.dtype)
    cu_ref[...] = hu[tm - 2:tm, :]
    cg_ref[...] = hg[tm - 2:tm, :]


def _ffn_up_prompt(x1b, w_up, w_conv, b_conv, conv_prev, tm):
    m = x1b.shape[0]
    col = lambda j, i: (0, j)
    colg = lambda j, i: (0, j + UP_NJ)
    return pl.pallas_call(
        functools.partial(_ffn_up_prompt_kernel, tm=tm),
        grid=(UP_NJ, m // tm),
        in_specs=[
            pl.BlockSpec((tm, D_MODEL), lambda j, i: (i, 0)),
            pl.BlockSpec((D_MODEL, UP_TN), col),
            pl.BlockSpec((D_MODEL, UP_TN), colg),
            pl.BlockSpec((CONV_W, UP_TN), col),
            pl.BlockSpec((CONV_W, UP_TN), colg),
            pl.BlockSpec((1, UP_TN), col),
            pl.BlockSpec((1, UP_TN), colg),
            pl.BlockSpec((CONV_W - 1, UP_TN), col),
            pl.BlockSpec((CONV_W - 1, UP_TN), colg),
        ],
        out_specs=[
            pl.BlockSpec((tm, UP_TN), lambda j, i: (i, j)),
            pl.BlockSpec((CONV_W - 1, UP_TN), col),
            pl.BlockSpec((CONV_W - 1, UP_TN), col),
        ],
        out_shape=[
            jax.ShapeDtypeStruct((m, D_FF_PAD), BF16),
            jax.ShapeDtypeStruct((CONV_W - 1, D_FF_PAD), F32),
            jax.ShapeDtypeStruct((CONV_W - 1, D_FF_PAD), F32),
        ],
        scratch_shapes=[pltpu.VMEM((tm + CARRY, UP_TN), F32), pltpu.VMEM((tm + CARRY, UP_TN), F32)],
        compiler_params=_params(("parallel", "arbitrary"), 40),
        name="ffn_up_prompt",
    )(x1b, w_up, w_up, w_conv, w_conv, b_conv, b_conv, conv_prev, conv_prev)


def _ffn_up_sample_kernel(x_ref, wu_ref, wg_ref, wcu_ref, wcg_ref, bcu_ref, bcg_ref,
                          a1u_ref, a2u_ref, a1g_ref, a2g_ref,
                          act_ref, hu_ref, hg_ref, bu_s, bg_s, *, seg):
    x = x_ref[...]
    hu = _dot(x, wu_ref[...])
    hg = _dot(x, wg_ref[...])
    bu_s[0:CARRY, :] = jnp.zeros((CARRY, UP_TN), F32)
    bg_s[0:CARRY, :] = jnp.zeros((CARRY, UP_TN), F32)
    cu = _conv_taps(bu_s, hu, wcu_ref, bcu_ref, (seg, a1u_ref[...], a2u_ref[...]))
    cg = _conv_taps(bg_s, hg, wcg_ref, bcg_ref, (seg, a1g_ref[...], a2g_ref[...]))
    act_ref[...] = (_gelu_tanh(cg) * cu).astype(act_ref.dtype)
    hu_ref[...] = hu
    hg_ref[...] = hg


def _ffn_up_sample(x1b, w_up, w_conv, b_conv, a1, a2, seg):
    m = x1b.shape[0]
    col = lambda j: (0, j)
    colg = lambda j: (0, j + UP_NJ)
    return pl.pallas_call(
        functools.partial(_ffn_up_sample_kernel, seg=seg),
        grid=(UP_NJ,),
        in_specs=[
            pl.BlockSpec((m, D_MODEL), lambda j: (0, 0)),
            pl.BlockSpec((D_MODEL, UP_TN), col),
            pl.BlockSpec((D_MODEL, UP_TN), colg),
            pl.BlockSpec((CONV_W, UP_TN), col),
            pl.BlockSpec((CONV_W, UP_TN), colg),
            pl.BlockSpec((1, UP_TN), col),
            pl.BlockSpec((1, UP_TN), colg),
            pl.BlockSpec((m, UP_TN), col),
            pl.BlockSpec((m, UP_TN), col),
            pl.BlockSpec((m, UP_TN), colg),
            pl.BlockSpec((m, UP_TN), colg),
        ],
        out_specs=[
            pl.BlockSpec((m, UP_TN), col),
            pl.BlockSpec((m, UP_TN), col),
            pl.BlockSpec((m, UP_TN), col),
        ],
        out_shape=[
            jax.ShapeDtypeStruct((m, D_FF_PAD), BF16),
            jax.ShapeDtypeStruct((m, D_FF_PAD), F32),
            jax.ShapeDtypeStruct((m, D_FF_PAD), F32),
        ],
        scratch_shapes=[pltpu.VMEM((m + CARRY, UP_TN), F32), pltpu.VMEM((m + CARRY, UP_TN), F32)],
        compiler_params=_params(("parallel",), 32),
        name="ffn_up_sample",
    )(x1b, w_up, w_up, w_conv, w_conv, b_conv, b_conv, a1, a2, a1, a2)


DOWN_TK = 512


def _ffn_down_kernel(a_ref, w_ref, x1_ref, g_ref, b_ref, y_ref, acc_ref):
    k = pl.program_id(1)

    @pl.when(k == 0)
    def _():
        acc_ref[...] = jnp.zeros_like(acc_ref)

    acc_ref[...] += _dot(a_ref[...], w_ref[...])

    @pl.when(k == pl.num_programs(1) - 1)
    def _():
        y_ref[...] = _layer_norm(DN_ALPHA * x1_ref[...] + acc_ref[...], g_ref[...], b_ref[...])


def _ffn_down(act, w_down, x1, ln_g, ln_b, tm):
    m = act.shape[0]
    return pl.pallas_call(
        _ffn_down_kernel,
        grid=(m // tm, D_FF_PAD // DOWN_TK),
        in_specs=[
            pl.BlockSpec((tm, DOWN_TK), lambda i, k: (i, k)),
            pl.BlockSpec((DOWN_TK, D_MODEL), lambda i, k: (k, 0)),
            pl.BlockSpec((tm, D_MODEL), lambda i, k: (i, 0)),
            pl.BlockSpec((1, D_MODEL), lambda i, k: (0, 0)),
            pl.BlockSpec((1, D_MODEL), lambda i, k: (0, 0)),
        ],
        out_specs=pl.BlockSpec((tm, D_MODEL), lambda i, k: (i, 0)),
        out_shape=jax.ShapeDtypeStruct((m, D_MODEL), F32),
        scratch_shapes=[pltpu.VMEM((tm, D_MODEL), F32)],
        compiler_params=_params(("parallel", "arbitrary"), 40),
        name="ffn_down_ln",
    )(act, w_down, x1, ln_g, ln_b)


def _pad_cols(a, width):
    return jnp.pad(a, ((0, 0), (0, width - a.shape[1])))


def _split_pad(a):
    return jnp.concatenate([_pad_cols(a[:, :D_FF], D_FF_PAD), _pad_cols(a[:, D_FF:], D_FF_PAD)], axis=1)


def _unsplit(u, g):
    return jnp.concatenate([u[..., :D_FF], g[..., :D_FF]], axis=-1)


def kernel(x_prompt, x_sample, cache_k, cache_v, state_C, state_n, state_m, state_conv, page_table,
           w_in, b_gates, b_sb, w_mlstm_norm, w_out, ln1_g, ln1_b, w_up, w_conv, b_conv, w_down,
           ln2_g, ln2_b):
    n_b, seq, _ = x_prompt.shape
    n_s, n_q, _ = x_sample.shape
    assert n_b == 1

    wi = w_in[0]
    w_main = jnp.concatenate([wi[:, :GATE_COL], wi[:, GATE_COL + 2 * H_M:]], axis=1).astype(BF16)
    w_gate = _pad_cols(wi[:, GATE_COL:GATE_COL + 2 * H_M], LANES).astype(BF16)
    bg = _pad_cols(b_gates[0][None, :], LANES)
    wn = w_mlstm_norm[0][None, :]
    wo = w_out[0].astype(BF16)
    wu = _split_pad(w_up[0]).astype(BF16)
    wc = _split_pad(w_conv[0])
    bc = _split_pad(b_conv[0][None, :])
    wd = jnp.pad(w_down[0], ((0, D_FF_PAD - D_FF), (0, 0))).astype(BF16)
    g1, b1 = ln1_g[0][None, :], ln1_b[0][None, :]
    g2, b2 = ln2_g[0][None, :], ln2_b[0][None, :]
    bsb = b_sb[0]

    xp = x_prompt.reshape(seq, D_MODEL)
    p_p, k_p, v_p, kb_p, vb_p, gt_p = _inproj(xp, w_main, w_gate, 512)
    hm_p, c_p, nn_p, mm_p = _mlstm(
        p_p, gt_p, bg, wn,
        jnp.zeros((1, H_M, DK_M, DV_M), F32), jnp.zeros((1, H_M, 1, DK_M), F32),
        jnp.zeros((1, H_M, 1, 1), F32), batch=1, L=256, out_dtype=BF16)
    hs_p = _sb_prompt(p_p, kb_p, vb_p, bsb)
    x1_p, x1b_p = _outproj(hm_p, hs_p, xp, wo, g1, b1, 256)
    act_p, cu_p, cg_p = _ffn_up_prompt(x1b_p, wu, wc, bc, jnp.zeros((CONV_W - 1, 2 * D_FF_PAD), F32), 512)
    y_p = _ffn_down(act_p, wd, x1_p, g2, b2, 512)

    xs = x_sample.reshape(n_s * n_q, D_MODEL)
    p_s, k_s, v_s, _, _, gt_s = _inproj(xs, w_main, w_gate, n_s * n_q)
    hm_s, c_s, nn_s, mm_s = _mlstm(
        p_s, gt_s, bg, wn, state_C[0], state_n[0][:, :, None, :], state_m[0][:, :, None, None],
        batch=n_s, L=n_q, out_dtype=F32)
    hs_s = _sb_sample(p_s, k_s, v_s, cache_k.reshape(-1, DH_SB), cache_v.reshape(-1, DH_SB),
                      page_table, bsb, n_q=n_q)
    x1_s, x1b_s = _outproj(hm_s, hs_s, xs, wo, g1, b1, n_s * n_q)
    st = _split_pad(state_conv[0].reshape(n_s * (CONV_W - 1), 2 * D_FF)).reshape(n_s, CONV_W - 1, 2 * D_FF_PAD)
    zeros = jnp.zeros((n_s, n_q, 2 * D_FF_PAD), F32)
    a1 = zeros.at[:, 0].set(st[:, 1]).reshape(n_s * n_q, 2 * D_FF_PAD)
    a2 = zeros.at[:, 0].set(st[:, 0]).at[:, 1].set(st[:, 1]).reshape(n_s * n_q, 2 * D_FF_PAD)
    act_s, hu_s, hg_s = _ffn_up_sample(x1b_s, wu, wc, bc, a1, a2, n_q)
    y_s = _ffn_down(act_s, wd, x1_s, g2, b2, n_s * n_q)

    conv_s = _unsplit(hu_s.reshape(n_s, n_q, D_FF_PAD)[:, n_q - 2:], hg_s.reshape(n_s, n_q, D_FF_PAD)[:, n_q - 2:])
    return (
        y_p.reshape(1, seq, D_MODEL),
        y_s.reshape(n_s, n_q, D_MODEL),
        k_p.reshape(1, 1, seq, H_SB, DH_SB),
        v_p.reshape(1, 1, seq, H_SB, DH_SB),
        c_p[None],
        nn_p.reshape(1, 1, H_M, DK_M),
        mm_p.reshape(1, 1, H_M),
        _unsplit(cu_p, cg_p)[None, None],
        k_s.reshape(1, n_s, n_q, H_SB, DH_SB),
        v_s.reshape(1, n_s, n_q, H_SB, DH_SB),
        c_s[None],
        nn_s.reshape(1, n_s, H_M, DK_M),
        mm_s.reshape(1, n_s, H_M),
        conv_s[None],
    )
```

```python
import functools

import jax
import jax.numpy as jnp
from jax import lax
from jax.experimental import pallas as pl
from jax.experimental.pallas import tpu as pltpu

F32 = jnp.float32
BF16 = jnp.bfloat16

D_MODEL = 2048
H_M = 4
DK_M = 128
DV_M = 256
W_M = H_M * DV_M
H_SB = 8
DH_SB = 128
W_SB = H_SB * DH_SB
D_FF = 5504
D_FF_PAD = 5632
CONV_W = 3
PAGE_SIZE = 128
LN_EPS = 1e-5
RMS_EPS = 1e-6
DN_ALPHA = 2.0 ** 0.25
GATE_COL = 3072
N_GATES = 2 * H_M
LANES = 128
MIB = 1024 * 1024
LOG2E = 1.4426950408889634


def _params(semantics, vmem_mib):
    return pltpu.CompilerParams(dimension_semantics=semantics,
                                vmem_limit_bytes=int(vmem_mib * MIB))


def _dot(a, b):
    return jnp.dot(a, b, preferred_element_type=F32)


def _dot_nt(a, b):
    return lax.dot_general(a, b, (((1,), (1,)), ((), ())), preferred_element_type=F32)


def _dot_tn(a, b):
    return lax.dot_general(a, b, (((0,), (0,)), ((), ())), preferred_element_type=F32)


def _softplus(z):
    return jnp.maximum(z, 0.0) + jnp.log1p(jnp.exp(-jnp.abs(z)))


def _softplus2(z):
    return jnp.maximum(z, 0.0) + jnp.log2(1.0 + jnp.exp2(-jnp.abs(z)))


IN_TN = 512
IN_NJ_A = GATE_COL // IN_TN
IN_NJ_P = 4096 // IN_TN
IN_NJ = IN_NJ_P + 4


def _inproj_kernel(x_ref, wa_ref, wb_ref, wg_ref, p_ref, k_ref, v_ref, kb_ref, vb_ref, g_ref, xb_ref):
    j = pl.program_id(1)

    @pl.when(j == 0)
    def _():
        xb = x_ref[...].astype(BF16)
        xb_ref[...] = xb
        g_ref[...] = _dot(xb, wg_ref[...])

    @pl.when(j < IN_NJ_A)
    def _():
        p_ref[...] = _dot(xb_ref[...], wa_ref[...])

    @pl.when(j >= IN_NJ_A)
    def _():
        acc = _dot(xb_ref[...], wb_ref[...])

        @pl.when(j < IN_NJ_P)
        def _():
            p_ref[...] = acc

        @pl.when((j >= IN_NJ_P) & (j < IN_NJ_P + 2))
        def _():
            k_ref[...] = acc
            kb_ref[...] = acc.astype(BF16)

        @pl.when(j >= IN_NJ_P + 2)
        def _():
            v_ref[...] = acc
            vb_ref[...] = acc.astype(BF16)


def _inproj(x, w_a, w_b, w_gate, tm):
    m = x.shape[0]
    k_blk = lambda i, j: (i, jnp.clip(j - IN_NJ_P, 0, 1))
    v_blk = lambda i, j: (i, jnp.clip(j - IN_NJ_P - 2, 0, 1))
    return pl.pallas_call(
        _inproj_kernel,
        grid=(m // tm, IN_NJ),
        in_specs=[
            pl.BlockSpec((tm, D_MODEL), lambda i, j: (i, 0)),
            pl.BlockSpec((D_MODEL, IN_TN), lambda i, j: (0, jnp.minimum(j, IN_NJ_A - 1))),
            pl.BlockSpec((D_MODEL, IN_TN), lambda i, j: (0, jnp.maximum(j - IN_NJ_A, 0))),
            pl.BlockSpec((D_MODEL, LANES), lambda i, j: (0, 0)),
        ],
        out_specs=[
            pl.BlockSpec((tm, IN_TN), lambda i, j: (i, jnp.minimum(j, IN_NJ_P - 1))),
            pl.BlockSpec((tm, IN_TN), k_blk),
            pl.BlockSpec((tm, IN_TN), v_blk),
            pl.BlockSpec((tm, IN_TN), k_blk),
            pl.BlockSpec((tm, IN_TN), v_blk),
            pl.BlockSpec((tm, LANES), lambda i, j: (i, 0)),
        ],
        out_shape=[
            jax.ShapeDtypeStruct((m, 4096), F32),
            jax.ShapeDtypeStruct((m, W_SB), F32),
            jax.ShapeDtypeStruct((m, W_SB), F32),
            jax.ShapeDtypeStruct((m, W_SB), BF16),
            jax.ShapeDtypeStruct((m, W_SB), BF16),
            jax.ShapeDtypeStruct((m, LANES), F32),
        ],
        scratch_shapes=[pltpu.VMEM((tm, D_MODEL), BF16)],
        compiler_params=_params(("parallel", "arbitrary"), 54),
        name="inproj",
    )(x, w_a, w_b, w_gate)


def _mlstm_kernel(q_ref, k_ref, v_ref, o_ref, g_ref, bg_ref, wn_ref, c0_ref, n0_ref, m0_ref,
                  h_ref, cout_ref, nout_ref, mout_ref, c_s, n_s, m_s, *, L):
    c = pl.program_id(1)

    @pl.when(c == 0)
    def _():
        c_s[...] = c0_ref[0]
        n_s[...] = n0_ref[0]
        m_s[...] = m0_ref[0]

    g = g_ref[...] + bg_ref[...]
    row = lax.broadcasted_iota(jnp.int32, (L, L), 0)
    col = lax.broadcasted_iota(jnp.int32, (L, L), 1)
    causal = col <= row
    eye = col == row
    for h in range(H_M):
        li_col = g[:, h:h + 1]
        fg_col = g[:, H_M + h:H_M + h + 1]
        lf_col = -_softplus(-fg_col)
        li_row = jnp.sum(jnp.where(eye, li_col, 0.0), axis=0, keepdims=True)
        lf_row = jnp.sum(jnp.where(eye, lf_col, 0.0), axis=0, keepdims=True)
        b_row = jnp.sum(jnp.where(row <= col, lf_col, 0.0), axis=0, keepdims=True)
        b_col = jnp.sum(jnp.where(causal, lf_row, 0.0), axis=1, keepdims=True)
        b_last = b_col[L - 1:L, :]
        m_prev = m_s[h]
        d = jnp.where(causal, b_col - b_row + li_row, -jnp.inf)
        a_col = b_col + m_prev
        m_t = jnp.maximum(a_col, jnp.max(d, axis=1, keepdims=True))
        w_intra = jnp.exp(d - m_t)
        w_inter = jnp.exp(a_col - m_t)

        qf = q_ref[:, h * DK_M:(h + 1) * DK_M] * (DK_M ** -0.5)
        kf = k_ref[:, h * DK_M:(h + 1) * DK_M]
        qb = qf.astype(BF16)
        kb = kf.astype(BF16)
        vb = v_ref[:, h * DV_M:(h + 1) * DV_M].astype(BF16)
        c_prev = c_s[h]
        n_prev = n_s[h]

        s = _dot_nt(qb, kb) * w_intra
        num = w_inter * _dot(qb, c_prev.astype(BF16)) + _dot(s.astype(BF16), vb)
        den = (w_inter * jnp.sum(qf * n_prev, axis=1, keepdims=True)
               + jnp.sum(s, axis=1, keepdims=True))
        hh = num / jnp.maximum(jnp.abs(den), jnp.exp(-m_t))

        m_new = m_t[L - 1:L, :]
        w_end = jnp.exp(b_last - b_col + li_col - m_new)
        decay = jnp.exp(b_last + m_prev - m_new)
        kw = kf * w_end
        c_s[h] = decay * c_prev + _dot_tn(kw.astype(BF16), vb)
        n_s[h] = decay * n_prev + jnp.sum(kw, axis=0, keepdims=True)
        m_s[h] = m_new

        hn = hh * lax.rsqrt(jnp.mean(hh * hh, axis=1, keepdims=True) + RMS_EPS)
        hn = hn * wn_ref[:, h * DV_M:(h + 1) * DV_M]
        og = jax.nn.sigmoid(o_ref[:, h * DV_M:(h + 1) * DV_M])
        h_ref[:, h * DV_M:(h + 1) * DV_M] = (og * hn).astype(h_ref.dtype)

    @pl.when(c == pl.num_programs(1) - 1)
    def _():
        cout_ref[0] = c_s[...]
        nout_ref[0] = n_s[...]
        mout_ref[0] = m_s[...]


def _mlstm(p, gates, bg, wn, c0, n0, m0, *, batch, L, out_dtype):
    t_total = p.shape[0]
    nc = t_total // (batch * L)
    row = lambda b, c: b * nc + c
    return pl.pallas_call(
        functools.partial(_mlstm_kernel, L=L),
        grid=(batch, nc),
        in_specs=[
            pl.BlockSpec((L, H_M * DK_M), lambda b, c: (row(b, c), 0)),
            pl.BlockSpec((L, H_M * DK_M), lambda b, c: (row(b, c), 1)),
            pl.BlockSpec((L, W_M), lambda b, c: (row(b, c), 1)),
            pl.BlockSpec((L, W_M), lambda b, c: (row(b, c), 2)),
            pl.BlockSpec((L, LANES), lambda b, c: (row(b, c), 0)),
            pl.BlockSpec((1, LANES), lambda b, c: (0, 0)),
            pl.BlockSpec((1, W_M), lambda b, c: (0, 0)),
            pl.BlockSpec((1, H_M, DK_M, DV_M), lambda b, c: (b, 0, 0, 0)),
            pl.BlockSpec((1, H_M, 1, DK_M), lambda b, c: (b, 0, 0, 0)),
            pl.BlockSpec((1, H_M, 1, 1), lambda b, c: (b, 0, 0, 0)),
        ],
        out_specs=[
            pl.BlockSpec((L, W_M), lambda b, c: (row(b, c), 0)),
            pl.BlockSpec((1, H_M, DK_M, DV_M), lambda b, c: (b, 0, 0, 0)),
            pl.BlockSpec((1, H_M, 1, DK_M), lambda b, c: (b, 0, 0, 0)),
            pl.BlockSpec((1, H_M, 1, 1), lambda b, c: (b, 0, 0, 0)),
        ],
        out_shape=[
            jax.ShapeDtypeStruct((t_total, W_M), out_dtype),
            jax.ShapeDtypeStruct((batch, H_M, DK_M, DV_M), F32),
            jax.ShapeDtypeStruct((batch, H_M, 1, DK_M), F32),
            jax.ShapeDtypeStruct((batch, H_M, 1, 1), F32),
        ],
        scratch_shapes=[
            pltpu.VMEM((H_M, DK_M, DV_M), F32),
            pltpu.VMEM((H_M, 1, DK_M), F32),
            pltpu.VMEM((H_M, 1, 1), F32),
        ],
        compiler_params=_params(("parallel", "arbitrary"), 32),
        name="mlstm",
    )(p, p, p, p, gates, bg, wn, c0, n0, m0)


SBP_BQ = 512
SBP_BK = 256
SBP_NH = 2
SB_QSCALE = (DH_SB ** -0.5) * LOG2E


def _sb_prompt_kernel(bsb_ref, q_ref, k_ref, v_ref, o_ref, acc_ref, rest_ref,
                      u0_ref, u1_ref, t0_ref, t1_ref):
    hp = pl.program_id(0)
    i = pl.program_id(1)
    bq, bk = SBP_BQ, SBP_BK
    per_q = bq // bk
    nk = per_q * (i + 1)
    heads = [slice(hh * DH_SB, (hh + 1) * DH_SB) for hh in range(SBP_NH)]
    qs = [(q_ref[:, hs] * SB_QSCALE).astype(BF16) for hs in heads]
    bias = [bsb_ref[hp * SBP_NH + hh] * LOG2E for hh in range(SBP_NH)]
    later_aug = _later_aug(bk)
    row = lax.broadcasted_iota(jnp.int32, (bq, bk), 0)
    col = lax.broadcasted_iota(jnp.int32, (bq, bk), 1)
    acc_ref[...] = jnp.zeros_like(acc_ref)
    rest_ref[...] = jnp.zeros_like(rest_ref)

    def key_rows(jj):
        return pl.ds(pl.multiple_of((nk - 1 - jj) * bk, bk), bk)

    def scores(jj, u_ref, t_ref, diag):
        rows = key_rows(jj)
        mask = None if diag is None else (col + bk * (per_q - 1 - diag)) < row
        for hh, hs in enumerate(heads):
            z = _dot_nt(qs[hh], k_ref[rows, hs]) + bias[hh]
            sp = _softplus2(z)
            spm = sp if mask is None else jnp.where(mask, sp, 0.0)
            ea = _dot(spm.astype(BF16), later_aug)
            u = z - sp - ea[:, :bk]
            u_ref[hh] = u if mask is None else jnp.where(mask, u, -jnp.inf)
            t_ref[hh] = ea[:, bk:]

    def values(jj, u_ref, t_ref):
        rows = key_rows(jj)
        for hh, hs in enumerate(heads):
            rest = rest_ref[hh]
            a = jnp.exp2(u_ref[hh] + jnp.concatenate([rest] * (bk // LANES), axis=1))
            acc_ref[hh] += _dot(a.astype(BF16), v_ref[rows, hs])
            rest_ref[hh] = rest - t_ref[hh]

    scores(0, u0_ref, t0_ref, 0)
    scores(1, u1_ref, t1_ref, 1)
    values(0, u0_ref, t0_ref)

    def pair(p, carry):
        jj = per_q * (p + 1)
        scores(jj, u0_ref, t0_ref, None)
        values(jj - 1, u1_ref, t1_ref)
        scores(jj + 1, u1_ref, t1_ref, None)
        values(jj, u0_ref, t0_ref)
        return carry

    lax.fori_loop(0, i, pair, 0)
    values(nk - 1, u1_ref, t1_ref)
    for hh, hs in enumerate(heads):
        o_ref[:, hs] = acc_ref[hh].astype(o_ref.dtype)


def _sb_prompt(p, kb, vb, b_sb):
    assert SBP_BQ == 2 * SBP_BK
    T = p.shape[0]
    width = SBP_NH * DH_SB
    q_col0 = 3072 // width
    u_shape = pltpu.VMEM((SBP_NH, SBP_BQ, SBP_BK), F32)
    t_shape = pltpu.VMEM((SBP_NH, SBP_BQ, LANES), F32)
    return pl.pallas_call(
        _sb_prompt_kernel,
        grid=(H_SB // SBP_NH, T // SBP_BQ),
        in_specs=[
            pl.BlockSpec(memory_space=pltpu.SMEM),
            pl.BlockSpec((SBP_BQ, width), lambda h, i: (i, q_col0 + h)),
            pl.BlockSpec((T, width), lambda h, i: (0, h)),
            pl.BlockSpec((T, width), lambda h, i: (0, h)),
        ],
        out_specs=pl.BlockSpec((SBP_BQ, width), lambda h, i: (i, h)),
        out_shape=jax.ShapeDtypeStruct((T, W_SB), BF16),
        scratch_shapes=[pltpu.VMEM((SBP_NH, SBP_BQ, DH_SB), F32), t_shape,
                        u_shape, u_shape, t_shape, t_shape],
        compiler_params=_params(("parallel", "arbitrary"), 40),
        name="sb_prompt",
    )(b_sb, p, kb, vb)


SBS_PPS = 8


def _later_aug(n):
    row = lax.broadcasted_iota(jnp.int32, (n, n + LANES), 0)
    col = lax.broadcasted_iota(jnp.int32, (n, n + LANES), 1)
    return ((row > col) | (col >= n)).astype(BF16)


def _sb_sample_kernel(pt_ref, bsb_ref, q_ref, kn_ref, vn_ref, *rest, n_q):
    page_refs = rest[:2 * SBS_PPS]
    o_ref, acc_ref, rest_ref = rest[2 * SBS_PPS:]
    g = pl.program_id(1)
    rows = H_SB * n_q

    qs = [(q_ref[:, h * DH_SB:(h + 1) * DH_SB] * SB_QSCALE).astype(BF16) for h in range(H_SB)]
    bias = jnp.concatenate([jnp.full((n_q, PAGE_SIZE), bsb_ref[h] * LOG2E, F32) for h in range(H_SB)],
                           axis=0)
    later_aug = _later_aug(PAGE_SIZE)

    def head_dots_nt(ks):
        return jnp.concatenate([_dot_nt(qs[h], ks[h]) for h in range(H_SB)], axis=0)

    def head_dots(ab, vs):
        return jnp.concatenate([_dot(ab[h * n_q:(h + 1) * n_q, :], vs[h]) for h in range(H_SB)], axis=0)

    @pl.when(g == 0)
    def _():
        pad = jnp.zeros((PAGE_SIZE - n_q, DH_SB), BF16)
        ks = [jnp.concatenate([kn_ref[:, h * DH_SB:(h + 1) * DH_SB].astype(BF16), pad], axis=0)
              for h in range(H_SB)]
        vs = [jnp.concatenate([vn_ref[:, h * DH_SB:(h + 1) * DH_SB].astype(BF16), pad], axis=0)
              for h in range(H_SB)]
        r = lax.broadcasted_iota(jnp.int32, (rows, PAGE_SIZE), 0)
        cidx = lax.broadcasted_iota(jnp.int32, (rows, PAGE_SIZE), 1)
        mask = cidx < (r % n_q)
        z = head_dots_nt(ks) + bias
        sp = _softplus2(z)
        ea = _dot(jnp.where(mask, sp, 0.0).astype(BF16), later_aug)
        a = jnp.where(mask, jnp.exp2(z - sp - ea[:, :PAGE_SIZE]), 0.0)
        acc_ref[...] = head_dots(a.astype(BF16), vs)
        rest_ref[...] = -ea[:, PAGE_SIZE:]

    def head_rows(refs, h):
        return jnp.concatenate([r[pl.ds(h, PAGE_SIZE, stride=H_SB), :] for r in refs],
                               axis=0).astype(BF16)

    ks = [head_rows(page_refs[:SBS_PPS], h) for h in range(H_SB)]
    vs = [head_rows(page_refs[SBS_PPS:], h) for h in range(H_SB)]
    z = head_dots_nt(ks) + jnp.concatenate([bias] * SBS_PPS, axis=1)
    sp = _softplus2(z)
    spb = sp.astype(BF16)
    rest_v = rest_ref[...]
    es, rests = [], []
    for c in range(SBS_PPS):
        ea = _dot(spb[:, c * PAGE_SIZE:(c + 1) * PAGE_SIZE], later_aug)
        es.append(ea[:, :PAGE_SIZE])
        rests.append(rest_v)
        rest_v = rest_v - ea[:, PAGE_SIZE:]
    a = jnp.exp2(z - sp - jnp.concatenate(es, axis=1) + jnp.concatenate(rests, axis=1))
    acc = acc_ref[...] + head_dots(a.astype(BF16), vs)
    acc_ref[...] = acc
    rest_ref[...] = rest_v

    @pl.when(g == pl.num_programs(1) - 1)
    def _():
        for h in range(H_SB):
            o_ref[:, h * DH_SB:(h + 1) * DH_SB] = acc[h * n_q:(h + 1) * n_q, :]


def _sb_sample(p, k_new, v_new, cache_k, cache_v, page_table, b_sb, *, n_q):
    n_seq, n_pages = page_table.shape
    n_groups = n_pages // SBS_PPS

    def page_spec(c):
        return pl.BlockSpec(
            (PAGE_SIZE * H_SB, DH_SB),
            lambda b, g, pt: (pt[b, n_pages - 1 - (g * SBS_PPS + c)], 0))

    return pl.pallas_call(
        functools.partial(_sb_sample_kernel, n_q=n_q),
        grid_spec=pltpu.PrefetchScalarGridSpec(
            num_scalar_prefetch=1,
            grid=(n_seq, n_groups),
            in_specs=[
                pl.BlockSpec(memory_space=pltpu.SMEM),
                pl.BlockSpec((n_q, W_SB), lambda b, g, pt: (b, 3)),
                pl.BlockSpec((n_q, W_SB), lambda b, g, pt: (b, 0)),
                pl.BlockSpec((n_q, W_SB), lambda b, g, pt: (b, 0)),
            ] + [page_spec(c) for c in range(SBS_PPS)] * 2,
            out_specs=pl.BlockSpec((n_q, W_SB), lambda b, g, pt: (b, 0)),
            scratch_shapes=[pltpu.VMEM((H_SB * n_q, DH_SB), F32),
                            pltpu.VMEM((H_SB * n_q, LANES), F32)],
        ),
        out_shape=jax.ShapeDtypeStruct((n_seq * n_q, W_SB), F32),
        compiler_params=_params(("parallel", "arbitrary"), 40),
        name="sb_sample",
    )(page_table, b_sb, p, k_new, v_new,
      *([cache_k] * SBS_PPS), *([cache_v] * SBS_PPS))


ROW_CHUNK = 256


def _layer_norm(x, g, b):
    mu = jnp.mean(x, axis=-1, keepdims=True)
    xc = x - mu
    var = jnp.mean(xc * xc, axis=-1, keepdims=True)
    return xc * lax.rsqrt(var + LN_EPS) * g + b


def _outproj_kernel(hm_ref, hs_ref, x_ref, wm_ref, ws_ref, g_ref, b_ref, x1_ref, x1b_ref, *, tm):
    for c in range(tm // ROW_CHUNK):
        rows = slice(c * ROW_CHUNK, (c + 1) * ROW_CHUNK)
        mix = (_dot(hm_ref[rows, :].astype(BF16), wm_ref[...])
               + _dot(hs_ref[rows, :].astype(BF16), ws_ref[...]))
        x1 = _layer_norm(DN_ALPHA * x_ref[rows, :] + mix, g_ref[...], b_ref[...])
        x1_ref[rows, :] = x1
        x1b_ref[rows, :] = x1.astype(BF16)


def _outproj(hm, hs, x, w_out, ln_g, ln_b, tm):
    m = x.shape[0]
    return pl.pallas_call(
        functools.partial(_outproj_kernel, tm=tm),
        grid=(m // tm,),
        in_specs=[
            pl.BlockSpec((tm, W_M), lambda i: (i, 0)),
            pl.BlockSpec((tm, W_SB), lambda i: (i, 0)),
            pl.BlockSpec((tm, D_MODEL), lambda i: (i, 0)),
            pl.BlockSpec((W_M, D_MODEL), lambda i: (0, 0)),
            pl.BlockSpec((W_SB, D_MODEL), lambda i: (1, 0)),
            pl.BlockSpec((1, D_MODEL), lambda i: (0, 0)),
            pl.BlockSpec((1, D_MODEL), lambda i: (0, 0)),
        ],
        out_specs=[
            pl.BlockSpec((tm, D_MODEL), lambda i: (i, 0)),
            pl.BlockSpec((tm, D_MODEL), lambda i: (i, 0)),
        ],
        out_shape=[
            jax.ShapeDtypeStruct((m, D_MODEL), F32),
            jax.ShapeDtypeStruct((m, D_MODEL), BF16),
        ],
        compiler_params=_params(("parallel",), 48),
        name="outproj_ln",
    )(hm, hs, x, w_out, w_out, ln_g, ln_b)


UP_TN = 512
UP_NJ = D_FF_PAD // UP_TN
CARRY = 8


def _gelu_tanh(x):
    return 0.5 * x * (1.0 + jnp.tanh(0.7978845608028654 * (x + 0.044715 * (x * x * x))))


def _conv_taps(buf_ref, h, off, wc_ref, bc_ref):
    n = h.shape[0]
    buf_ref[CARRY + off:CARRY + off + n, :] = h
    h1 = buf_ref[CARRY + off - 1:CARRY + off - 1 + n, :]
    h2 = buf_ref[CARRY + off - 2:CARRY + off - 2 + n, :]
    return bc_ref[...] + wc_ref[0:1, :] * h2 + wc_ref[1:2, :] * h1 + wc_ref[2:3, :] * h


def _ffn_up_prompt_kernel(x_ref, wu_ref, wg_ref, wcu_ref, wcg_ref, bcu_ref, bcg_ref, pu_ref, pg_ref,
                          act_ref, cu_ref, cg_ref, bu_s, bg_s, *, tm):
    i = pl.program_id(1)

    @pl.when(i == 0)
    def _():
        bu_s[CARRY - 2:CARRY, :] = pu_ref[...]
        bg_s[CARRY - 2:CARRY, :] = pg_ref[...]

    @pl.when(i > 0)
    def _():
        bu_s[0:CARRY, :] = bu_s[tm:tm + CARRY, :]
        bg_s[0:CARRY, :] = bg_s[tm:tm + CARRY, :]

    for c in range(tm // ROW_CHUNK):
        off = c * ROW_CHUNK
        x = x_ref[off:off + ROW_CHUNK, :]
        hu = _dot(x, wu_ref[...])
        hg = _dot(x, wg_ref[...])
        cu = _conv_taps(bu_s, hu, off, wcu_ref, bcu_ref)
        cg = _conv_taps(bg_s, hg, off, wcg_ref, bcg_ref)
        act_ref[off:off + ROW_CHUNK, :] = (_gelu_tanh(cg) * cu).astype(act_ref.dtype)
    cu_ref[...] = hu[ROW_CHUNK - 2:ROW_CHUNK, :]
    cg_ref[...] = hg[ROW_CHUNK - 2:ROW_CHUNK, :]


def _ffn_up_prompt(x1b, wu, wg, wcu, wcg, bcu, bcg, pu, pg, tm):
    m = x1b.shape[0]
    col = lambda j, i: (0, j)
    return pl.pallas_call(
        functools.partial(_ffn_up_prompt_kernel, tm=tm),
        grid=(UP_NJ, m // tm),
        in_specs=[
            pl.BlockSpec((tm, D_MODEL), lambda j, i: (i, 0)),
            pl.BlockSpec((D_MODEL, UP_TN), col),
            pl.BlockSpec((D_MODEL, UP_TN), col),
            pl.BlockSpec((CONV_W, UP_TN), col),
            pl.BlockSpec((CONV_W, UP_TN), col),
            pl.BlockSpec((1, UP_TN), col),
            pl.BlockSpec((1, UP_TN), col),
            pl.BlockSpec((CONV_W - 1, UP_TN), col),
            pl.BlockSpec((CONV_W - 1, UP_TN), col),
        ],
        out_specs=[
            pl.BlockSpec((tm, UP_TN), lambda j, i: (i, j)),
            pl.BlockSpec((CONV_W - 1, UP_TN), col),
            pl.BlockSpec((CONV_W - 1, UP_TN), col),
        ],
        out_shape=[
            jax.ShapeDtypeStruct((m, D_FF_PAD), BF16),
            jax.ShapeDtypeStruct((CONV_W - 1, D_FF_PAD), F32),
            jax.ShapeDtypeStruct((CONV_W - 1, D_FF_PAD), F32),
        ],
        scratch_shapes=[pltpu.VMEM((tm + CARRY, UP_TN), F32), pltpu.VMEM((tm + CARRY, UP_TN), F32)],
        compiler_params=_params(("parallel", "arbitrary"), 40),
        name="ffn_up_prompt",
    )(x1b, wu, wg, wcu, wcg, bcu, bcg, pu, pg)


def _conv_taps_seq(buf_ref, h, s0_ref, s1_ref, wc_ref, bc_ref, seg):
    n = h.shape[0]
    n_seq = n // seg
    h1s, h2s = [], []
    for s in range(h.shape[1] // LANES):
        lanes = slice(s * LANES, (s + 1) * LANES)
        buf_ref[s, CARRY:CARRY + n, :] = h[:, lanes]
        buf_ref[s, pl.ds(CARRY - 1, n_seq, stride=seg), :] = s1_ref[:, lanes]
        h1s.append(buf_ref[s, CARRY - 1:CARRY - 1 + n, :])
        buf_ref[s, pl.ds(CARRY - 2, n_seq, stride=seg), :] = s0_ref[:, lanes]
        h2s.append(buf_ref[s, CARRY - 2:CARRY - 2 + n, :])
    h1 = jnp.concatenate(h1s, axis=1)
    h2 = jnp.concatenate(h2s, axis=1)
    return bc_ref[...] + wc_ref[0:1, :] * h2 + wc_ref[1:2, :] * h1 + wc_ref[2:3, :] * h


def _ffn_up_sample_kernel(x_ref, wu_ref, wg_ref, wcu_ref, wcg_ref, bcu_ref, bcg_ref,
                          s0u_ref, s1u_ref, s0g_ref, s1g_ref,
                          act_ref, hu_ref, hg_ref, bu_s, bg_s, *, seg):
    x = x_ref[...]
    hu = _dot(x, wu_ref[...])
    hg = _dot(x, wg_ref[...])
    cu = _conv_taps_seq(bu_s, hu, s0u_ref, s1u_ref, wcu_ref, bcu_ref, seg)
    cg = _conv_taps_seq(bg_s, hg, s0g_ref, s1g_ref, wcg_ref, bcg_ref, seg)
    act_ref[...] = (_gelu_tanh(cg) * cu).astype(act_ref.dtype)
    hu_ref[...] = hu
    hg_ref[...] = hg


def _ffn_up_sample(x1b, wu, wg, wcu, wcg, bcu, bcg, s0u, s1u, s0g, s1g, seg):
    m = x1b.shape[0]
    n_seq = m // seg
    col = lambda j: (0, j)
    w_spec = pl.BlockSpec((D_MODEL, UP_TN), col)
    c_spec = pl.BlockSpec((CONV_W, UP_TN), col)
    b_spec = pl.BlockSpec((1, UP_TN), col)
    s_spec = pl.BlockSpec((n_seq, UP_TN), col)
    o_spec = pl.BlockSpec((m, UP_TN), col)
    return pl.pallas_call(
        functools.partial(_ffn_up_sample_kernel, seg=seg),
        grid=(UP_NJ,),
        in_specs=[pl.BlockSpec((m, D_MODEL), lambda j: (0, 0)), w_spec, w_spec, c_spec, c_spec,
                  b_spec, b_spec, s_spec, s_spec, s_spec, s_spec],
        out_specs=[o_spec, o_spec, o_spec],
        out_shape=[
            jax.ShapeDtypeStruct((m, D_FF_PAD), BF16),
            jax.ShapeDtypeStruct((m, D_FF_PAD), F32),
            jax.ShapeDtypeStruct((m, D_FF_PAD), F32),
        ],
        scratch_shapes=[pltpu.VMEM((UP_TN // LANES, m + CARRY, LANES), F32)] * 2,
        compiler_params=_params(("parallel",), 32),
        name="ffn_up_sample",
    )(x1b, wu, wg, wcu, wcg, bcu, bcg, s0u, s1u, s0g, s1g)


DOWN_TK = 1408
DOWN_NK = D_FF_PAD // DOWN_TK
DOWN_LAST = D_FF - (DOWN_NK - 1) * DOWN_TK


def _ffn_down_kernel(a_ref, w_ref, x1_ref, g_ref, b_ref, y_ref, acc_ref):
    k = pl.program_id(1)

    @pl.when(k == 0)
    def _():
        acc_ref[...] = _dot(a_ref[...], w_ref[...])

    @pl.when((k > 0) & (k < DOWN_NK - 1))
    def _():
        acc_ref[...] += _dot(a_ref[...], w_ref[...])

    @pl.when(k == DOWN_NK - 1)
    def _():
        ffn = acc_ref[...] + _dot(a_ref[:, :DOWN_LAST], w_ref[:DOWN_LAST, :])
        y_ref[...] = _layer_norm(DN_ALPHA * x1_ref[...] + ffn, g_ref[...], b_ref[...])


def _ffn_down(act, w_down, x1, ln_g, ln_b, tm):
    m = act.shape[0]
    return pl.pallas_call(
        _ffn_down_kernel,
        grid=(m // tm, DOWN_NK),
        in_specs=[
            pl.BlockSpec((tm, DOWN_TK), lambda i, k: (i, k)),
            pl.BlockSpec((DOWN_TK, D_MODEL), lambda i, k: (k, 0)),
            pl.BlockSpec((tm, D_MODEL), lambda i, k: (i, 0)),
            pl.BlockSpec((1, D_MODEL), lambda i, k: (0, 0)),
            pl.BlockSpec((1, D_MODEL), lambda i, k: (0, 0)),
        ],
        out_specs=pl.BlockSpec((tm, D_MODEL), lambda i, k: (i, 0)),
        out_shape=jax.ShapeDtypeStruct((m, D_MODEL), F32),
        scratch_shapes=[pltpu.VMEM((tm, D_MODEL), F32)],
        compiler_params=_params(("parallel", "arbitrary"), 48),
        name="ffn_down_ln",
    )(act, w_down, x1, ln_g, ln_b)


def _pad_cols(a, width):
    return jnp.pad(a, ((0, 0), (0, width - a.shape[1])))


def _halves(a, dtype=F32):
    return (_pad_cols(a[:, :D_FF], D_FF_PAD).astype(dtype), _pad_cols(a[:, D_FF:], D_FF_PAD).astype(dtype))


def _unsplit(u, g):
    return jnp.concatenate([u[..., :D_FF], g[..., :D_FF]], axis=-1)


def kernel(x_prompt, x_sample, cache_k, cache_v, state_C, state_n, state_m, state_conv, page_table,
           w_in, b_gates, b_sb, w_mlstm_norm, w_out, ln1_g, ln1_b, w_up, w_conv, b_conv, w_down,
           ln2_g, ln2_b):
    n_b, seq, _ = x_prompt.shape
    n_s, n_q, _ = x_sample.shape
    assert n_b == 1

    wi = w_in[0]
    w_a = wi[:, :GATE_COL].astype(BF16)
    w_b = wi[:, GATE_COL + N_GATES:].astype(BF16)
    w_gate = _pad_cols(wi[:, GATE_COL:GATE_COL + N_GATES], LANES).astype(BF16)
    bg = _pad_cols(b_gates[0][None, :], LANES)
    wn = w_mlstm_norm[0][None, :]
    wo = w_out[0].astype(BF16)
    wu, wg = _halves(w_up[0], BF16)
    wcu, wcg = _halves(w_conv[0])
    bcu, bcg = _halves(b_conv[0][None, :])
    wd = w_down[0].astype(BF16)
    g1, b1 = ln1_g[0][None, :], ln1_b[0][None, :]
    g2, b2 = ln2_g[0][None, :], ln2_b[0][None, :]
    bsb = b_sb[0]

    xp = x_prompt.reshape(seq, D_MODEL)
    p_p, k_p, v_p, kb_p, vb_p, gt_p = _inproj(xp, w_a, w_b, w_gate, 1024)
    hm_p, c_p, nn_p, mm_p = _mlstm(
        p_p, gt_p, bg, wn,
        jnp.zeros((1, H_M, DK_M, DV_M), F32), jnp.zeros((1, H_M, 1, DK_M), F32),
        jnp.zeros((1, H_M, 1, 1), F32), batch=1, L=256, out_dtype=BF16)
    hs_p = _sb_prompt(p_p, kb_p, vb_p, bsb)
    x1_p, x1b_p = _outproj(hm_p, hs_p, xp, wo, g1, b1, 512)
    conv0 = jnp.zeros((CONV_W - 1, D_FF_PAD), F32)
    act_p, cu_p, cg_p = _ffn_up_prompt(x1b_p, wu, wg, wcu, wcg, bcu, bcg, conv0, conv0, 1024)
    y_p = _ffn_down(act_p, wd, x1_p, g2, b2, 512)

    xs = x_sample.reshape(n_s * n_q, D_MODEL)
    p_s, k_s, v_s, _, _, gt_s = _inproj(xs, w_a, w_b, w_gate, n_s * n_q)
    hm_s, c_s, nn_s, mm_s = _mlstm(
        p_s, gt_s, bg, wn, state_C[0], state_n[0][:, :, None, :], state_m[0][:, :, None, None],
        batch=n_s, L=n_q, out_dtype=F32)
    hs_s = _sb_sample(p_s, k_s, v_s, cache_k.reshape(-1, DH_SB), cache_v.reshape(-1, DH_SB),
                      page_table, bsb, n_q=n_q)
    x1_s, x1b_s = _outproj(hm_s, hs_s, xs, wo, g1, b1, n_s * n_q)
    s0u, s0g = _halves(state_conv[0][:, 0, :])
    s1u, s1g = _halves(state_conv[0][:, 1, :])
    act_s, hu_s, hg_s = _ffn_up_sample(x1b_s, wu, wg, wcu, wcg, bcu, bcg, s0u, s1u, s0g, s1g, n_q)
    y_s = _ffn_down(act_s, wd, x1_s, g2, b2, n_s * n_q)

    conv_s = _unsplit(hu_s.reshape(n_s, n_q, D_FF_PAD)[:, n_q - 2:], hg_s.reshape(n_s, n_q, D_FF_PAD)[:, n_q - 2:])
    return (
        y_p.reshape(1, seq, D_MODEL),
        y_s.reshape(n_s, n_q, D_MODEL),
        k_p.reshape(1, 1, seq, H_SB, DH_SB),
        v_p.reshape(1, 1, seq, H_SB, DH_SB),
        c_p[None],
        nn_p.reshape(1, 1, H_M, DK_M),
        mm_p.reshape(1, 1, H_M),
        _unsplit(cu_p, cg_p)[None, None],
        k_s.reshape(1, n_s, n_q, H_SB, DH_SB),
        v_s.reshape(1, n_s, n_q, H_SB, DH_SB),
        c_s[None],
        nn_s.reshape(1, n_s, H_M, DK_M),
        mm_s.reshape(1, n_s, H_M),
        conv_s[None],
    )
```

```python
import functools

import jax
import jax.numpy as jnp
from jax import lax
from jax.experimental import pallas as pl
from jax.experimental.pallas import tpu as pltpu

F32 = jnp.float32
BF16 = jnp.bfloat16

D_MODEL = 2048
H_M = 4
DK_M = 128
DV_M = 256
W_M = H_M * DV_M
H_SB = 8
DH_SB = 128
W_SB = H_SB * DH_SB
D_FF = 5504
D_FF_PAD = 5632
CONV_W = 3
PAGE_SIZE = 128
LN_EPS = 1e-5
RMS_EPS = 1e-6
DN_ALPHA = 2.0 ** 0.25
GATE_COL = 3072
N_GATES = 2 * H_M
LANES = 128
MIB = 1024 * 1024
LOG2E = 1.4426950408889634


def _params(semantics, vmem_mib):
    return pltpu.CompilerParams(dimension_semantics=semantics,
                                vmem_limit_bytes=int(vmem_mib * MIB))


def _dot(a, b):
    return jnp.dot(a, b, preferred_element_type=F32)


def _dot_nt(a, b):
    return lax.dot_general(a, b, (((1,), (1,)), ((), ())), preferred_element_type=F32)


def _dot_tn(a, b):
    return lax.dot_general(a, b, (((0,), (0,)), ((), ())), preferred_element_type=F32)


def _softplus(z):
    return jnp.maximum(z, 0.0) + jnp.log1p(jnp.exp(-jnp.abs(z)))


def _softplus2(z):
    return jnp.maximum(z, 0.0) + jnp.log2(1.0 + jnp.exp2(-jnp.abs(z)))


PREP_ROWS = 128


def _prep_in_kernel(w_ref, wa_ref, wb_ref, wg_ref):
    w = w_ref[...]
    wa_ref[...] = w[:, :GATE_COL].astype(BF16)
    wb_ref[...] = w[:, GATE_COL + N_GATES:].astype(BF16)
    gates = jnp.concatenate([w[:, GATE_COL:GATE_COL + N_GATES],
                             jnp.zeros((w.shape[0], LANES - N_GATES), F32)], axis=1)
    wg_ref[...] = gates.astype(BF16)


def _prep_in(w_in):
    d, d_in = w_in.shape
    n_b = d_in - GATE_COL - N_GATES
    row = lambda r: (r, 0)
    return pl.pallas_call(
        _prep_in_kernel,
        grid=(d // PREP_ROWS,),
        in_specs=[pl.BlockSpec((PREP_ROWS, d_in), row)],
        out_specs=[pl.BlockSpec((PREP_ROWS, GATE_COL), row), pl.BlockSpec((PREP_ROWS, n_b), row),
                   pl.BlockSpec((PREP_ROWS, LANES), row)],
        out_shape=[jax.ShapeDtypeStruct((d, GATE_COL), BF16), jax.ShapeDtypeStruct((d, n_b), BF16),
                   jax.ShapeDtypeStruct((d, LANES), BF16)],
        compiler_params=_params(("parallel",), 24),
        name="prep_w_in",
    )(w_in)


def _prep_up_kernel(u_ref, g_ref, wu_ref, wg_ref):
    pad = jnp.zeros((u_ref.shape[0], D_FF_PAD - D_FF), BF16)
    wu_ref[:, :D_FF] = u_ref[...].astype(BF16)
    wu_ref[:, D_FF:] = pad
    wg_ref[:, :D_FF] = g_ref[...].astype(BF16)
    wg_ref[:, D_FF:] = pad


def _prep_up(w_up):
    d = w_up.shape[0]
    return pl.pallas_call(
        _prep_up_kernel,
        grid=(d // PREP_ROWS,),
        in_specs=[pl.BlockSpec((PREP_ROWS, D_FF), lambda r: (r, 0)),
                  pl.BlockSpec((PREP_ROWS, D_FF), lambda r: (r, 1))],
        out_specs=[pl.BlockSpec((PREP_ROWS, D_FF_PAD), lambda r: (r, 0))] * 2,
        out_shape=[jax.ShapeDtypeStruct((d, D_FF_PAD), BF16)] * 2,
        compiler_params=_params(("parallel",), 24),
        name="prep_w_up",
    )(w_up, w_up)


IN_TN = 512
IN_NJ_A = GATE_COL // IN_TN
IN_NJ_P = 4096 // IN_TN
IN_NJ = IN_NJ_P + 4


def _inproj_kernel(x_ref, wa_ref, wb_ref, wg_ref, p_ref, k_ref, v_ref, kb_ref, vb_ref, g_ref, xb_ref):
    j = pl.program_id(1)

    @pl.when(j == 0)
    def _():
        xb = x_ref[...].astype(BF16)
        xb_ref[...] = xb
        g_ref[...] = _dot(xb, wg_ref[...])

    @pl.when(j < IN_NJ_A)
    def _():
        p_ref[...] = _dot(xb_ref[...], wa_ref[...])

    @pl.when(j >= IN_NJ_A)
    def _():
        acc = _dot(xb_ref[...], wb_ref[...])

        @pl.when(j < IN_NJ_P)
        def _():
            p_ref[...] = acc

        @pl.when((j >= IN_NJ_P) & (j < IN_NJ_P + 2))
        def _():
            k_ref[...] = acc
            kb_ref[...] = acc.astype(BF16)

        @pl.when(j >= IN_NJ_P + 2)
        def _():
            v_ref[...] = acc
            vb_ref[...] = acc.astype(BF16)


def _inproj(x, w_a, w_b, w_gate, tm):
    m = x.shape[0]
    k_blk = lambda i, j: (i, jnp.clip(j - IN_NJ_P, 0, 1))
    v_blk = lambda i, j: (i, jnp.clip(j - IN_NJ_P - 2, 0, 1))
    return pl.pallas_call(
        _inproj_kernel,
        grid=(m // tm, IN_NJ),
        in_specs=[
            pl.BlockSpec((tm, D_MODEL), lambda i, j: (i, 0)),
            pl.BlockSpec((D_MODEL, IN_TN), lambda i, j: (0, jnp.minimum(j, IN_NJ_A - 1))),
            pl.BlockSpec((D_MODEL, IN_TN), lambda i, j: (0, jnp.maximum(j - IN_NJ_A, 0))),
            pl.BlockSpec((D_MODEL, LANES), lambda i, j: (0, 0)),
        ],
        out_specs=[
            pl.BlockSpec((tm, IN_TN), lambda i, j: (i, jnp.minimum(j, IN_NJ_P - 1))),
            pl.BlockSpec((tm, IN_TN), k_blk),
            pl.BlockSpec((tm, IN_TN), v_blk),
            pl.BlockSpec((tm, IN_TN), k_blk),
            pl.BlockSpec((tm, IN_TN), v_blk),
            pl.BlockSpec((tm, LANES), lambda i, j: (i, 0)),
        ],
        out_shape=[
            jax.ShapeDtypeStruct((m, 4096), F32),
            jax.ShapeDtypeStruct((m, W_SB), F32),
            jax.ShapeDtypeStruct((m, W_SB), F32),
            jax.ShapeDtypeStruct((m, W_SB), BF16),
            jax.ShapeDtypeStruct((m, W_SB), BF16),
            jax.ShapeDtypeStruct((m, LANES), F32),
        ],
        scratch_shapes=[pltpu.VMEM((tm, D_MODEL), BF16)],
        compiler_params=_params(("parallel", "arbitrary"), 54),
        name="inproj",
    )(x, w_a, w_b, w_gate)


def _mlstm_kernel(q_ref, k_ref, v_ref, o_ref, g_ref, bg_ref, wn_ref, c0_ref, n0_ref, m0_ref,
                  h_ref, cout_ref, nout_ref, mout_ref, c_s, n_s, m_s, *, L):
    c = pl.program_id(1)

    @pl.when(c == 0)
    def _():
        c_s[...] = c0_ref[0]
        n_s[...] = n0_ref[0]
        m_s[...] = m0_ref[0]

    g = g_ref[...] + bg_ref[...]
    row = lax.broadcasted_iota(jnp.int32, (L, L), 0)
    col = lax.broadcasted_iota(jnp.int32, (L, L), 1)
    causal = col <= row
    eye = col == row
    for h in range(H_M):
        li_col = g[:, h:h + 1]
        fg_col = g[:, H_M + h:H_M + h + 1]
        lf_col = -_softplus(-fg_col)
        li_row = jnp.sum(jnp.where(eye, li_col, 0.0), axis=0, keepdims=True)
        lf_row = jnp.sum(jnp.where(eye, lf_col, 0.0), axis=0, keepdims=True)
        b_row = jnp.sum(jnp.where(row <= col, lf_col, 0.0), axis=0, keepdims=True)
        b_col = jnp.sum(jnp.where(causal, lf_row, 0.0), axis=1, keepdims=True)
        b_last = b_col[L - 1:L, :]
        m_prev = m_s[h]
        d = jnp.where(causal, b_col - b_row + li_row, -jnp.inf)
        a_col = b_col + m_prev
        m_t = jnp.maximum(a_col, jnp.max(d, axis=1, keepdims=True))
        w_intra = jnp.exp(d - m_t)
        w_inter = jnp.exp(a_col - m_t)

        qf = q_ref[:, h * DK_M:(h + 1) * DK_M] * (DK_M ** -0.5)
        kf = k_ref[:, h * DK_M:(h + 1) * DK_M]
        qb = qf.astype(BF16)
        kb = kf.astype(BF16)
        vb = v_ref[:, h * DV_M:(h + 1) * DV_M].astype(BF16)
        c_prev = c_s[h]
        n_prev = n_s[h]

        s = _dot_nt(qb, kb) * w_intra
        num = w_inter * _dot(qb, c_prev.astype(BF16)) + _dot(s.astype(BF16), vb)
        den = (w_inter * jnp.sum(qf * n_prev, axis=1, keepdims=True)
               + jnp.sum(s, axis=1, keepdims=True))
        hh = num / jnp.maximum(jnp.abs(den), jnp.exp(-m_t))

        m_new = m_t[L - 1:L, :]
        w_end = jnp.exp(b_last - b_col + li_col - m_new)
        decay = jnp.exp(b_last + m_prev - m_new)
        kw = kf * w_end
        c_s[h] = decay * c_prev + _dot_tn(kw.astype(BF16), vb)
        n_s[h] = decay * n_prev + jnp.sum(kw, axis=0, keepdims=True)
        m_s[h] = m_new

        hn = hh * lax.rsqrt(jnp.mean(hh * hh, axis=1, keepdims=True) + RMS_EPS)
        hn = hn * wn_ref[:, h * DV_M:(h + 1) * DV_M]
        og = jax.nn.sigmoid(o_ref[:, h * DV_M:(h + 1) * DV_M])
        h_ref[:, h * DV_M:(h + 1) * DV_M] = (og * hn).astype(h_ref.dtype)

    @pl.when(c == pl.num_programs(1) - 1)
    def _():
        cout_ref[0] = c_s[...]
        nout_ref[0] = n_s[...]
        mout_ref[0] = m_s[...]


def _mlstm(p, gates, bg, wn, c0, n0, m0, *, batch, L, out_dtype):
    t_total = p.shape[0]
    nc = t_total // (batch * L)
    row = lambda b, c: b * nc + c
    return pl.pallas_call(
        functools.partial(_mlstm_kernel, L=L),
        grid=(batch, nc),
        in_specs=[
            pl.BlockSpec((L, H_M * DK_M), lambda b, c: (row(b, c), 0)),
            pl.BlockSpec((L, H_M * DK_M), lambda b, c: (row(b, c), 1)),
            pl.BlockSpec((L, W_M), lambda b, c: (row(b, c), 1)),
            pl.BlockSpec((L, W_M), lambda b, c: (row(b, c), 2)),
            pl.BlockSpec((L, LANES), lambda b, c: (row(b, c), 0)),
            pl.BlockSpec((1, LANES), lambda b, c: (0, 0)),
            pl.BlockSpec((1, W_M), lambda b, c: (0, 0)),
            pl.BlockSpec((1, H_M, DK_M, DV_M), lambda b, c: (b, 0, 0, 0)),
            pl.BlockSpec((1, H_M, 1, DK_M), lambda b, c: (b, 0, 0, 0)),
            pl.BlockSpec((1, H_M, 1, 1), lambda b, c: (b, 0, 0, 0)),
        ],
        out_specs=[
            pl.BlockSpec((L, W_M), lambda b, c: (row(b, c), 0)),
            pl.BlockSpec((1, H_M, DK_M, DV_M), lambda b, c: (b, 0, 0, 0)),
            pl.BlockSpec((1, H_M, 1, DK_M), lambda b, c: (b, 0, 0, 0)),
            pl.BlockSpec((1, H_M, 1, 1), lambda b, c: (b, 0, 0, 0)),
        ],
        out_shape=[
            jax.ShapeDtypeStruct((t_total, W_M), out_dtype),
            jax.ShapeDtypeStruct((batch, H_M, DK_M, DV_M), F32),
            jax.ShapeDtypeStruct((batch, H_M, 1, DK_M), F32),
            jax.ShapeDtypeStruct((batch, H_M, 1, 1), F32),
        ],
        scratch_shapes=[
            pltpu.VMEM((H_M, DK_M, DV_M), F32),
            pltpu.VMEM((H_M, 1, DK_M), F32),
            pltpu.VMEM((H_M, 1, 1), F32),
        ],
        compiler_params=_params(("parallel", "arbitrary"), 32),
        name="mlstm",
    )(p, p, p, p, gates, bg, wn, c0, n0, m0)


SBP_BQ = 512
SBP_BK = 256
SBP_NH = 4
SB_QSCALE = (DH_SB ** -0.5) * LOG2E


def _sb_prompt_kernel(bsb_ref, q_ref, k_ref, v_ref, o_ref, acc_ref, rest_ref,
                      u0_ref, u1_ref, t0_ref, t1_ref):
    hp = pl.program_id(0)
    i = pl.program_id(1)
    bq, bk = SBP_BQ, SBP_BK
    per_q = bq // bk
    nk = per_q * (i + 1)
    heads = [slice(hh * DH_SB, (hh + 1) * DH_SB) for hh in range(SBP_NH)]
    qs = [(q_ref[:, hs] * SB_QSCALE).astype(BF16) for hs in heads]
    bias = [bsb_ref[hp * SBP_NH + hh] * LOG2E for hh in range(SBP_NH)]
    later_aug = _later_aug(bk)
    row = lax.broadcasted_iota(jnp.int32, (bq, bk), 0)
    col = lax.broadcasted_iota(jnp.int32, (bq, bk), 1)
    acc_ref[...] = jnp.zeros_like(acc_ref)
    rest_ref[...] = jnp.zeros_like(rest_ref)

    def key_rows(jj):
        return pl.ds(pl.multiple_of((nk - 1 - jj) * bk, bk), bk)

    def scores(jj, u_ref, t_ref, diag):
        rows = key_rows(jj)
        mask = None if diag is None else (col + bk * (per_q - 1 - diag)) < row
        for hh, hs in enumerate(heads):
            z = _dot_nt(qs[hh], k_ref[rows, hs]) + bias[hh]
            sp = _softplus2(z)
            spm = sp if mask is None else jnp.where(mask, sp, 0.0)
            ea = _dot(spm.astype(BF16), later_aug)
            u = z - sp - ea[:, :bk]
            u_ref[hh] = u if mask is None else jnp.where(mask, u, -jnp.inf)
            t_ref[hh] = ea[:, bk:]

    def values(jj, u_ref, t_ref):
        rows = key_rows(jj)
        for hh, hs in enumerate(heads):
            rest = rest_ref[hh]
            a = jnp.exp2(u_ref[hh] + jnp.concatenate([rest] * (bk // LANES), axis=1))
            acc_ref[hh] += _dot(a.astype(BF16), v_ref[rows, hs])
            rest_ref[hh] = rest - t_ref[hh]

    scores(0, u0_ref, t0_ref, 0)
    scores(1, u1_ref, t1_ref, 1)
    values(0, u0_ref, t0_ref)

    def pair(p, carry):
        jj = per_q * (p + 1)
        scores(jj, u0_ref, t0_ref, None)
        values(jj - 1, u1_ref, t1_ref)
        scores(jj + 1, u1_ref, t1_ref, None)
        values(jj, u0_ref, t0_ref)
        return carry

    lax.fori_loop(0, i, pair, 0)
    values(nk - 1, u1_ref, t1_ref)
    for hh, hs in enumerate(heads):
        o_ref[:, hs] = acc_ref[hh].astype(o_ref.dtype)


def _sb_prompt(p, kb, vb, b_sb):
    assert SBP_BQ == 2 * SBP_BK
    T = p.shape[0]
    width = SBP_NH * DH_SB
    q_col0 = 3072 // width
    u_shape = pltpu.VMEM((SBP_NH, SBP_BQ, SBP_BK), F32)
    t_shape = pltpu.VMEM((SBP_NH, SBP_BQ, LANES), F32)
    return pl.pallas_call(
        _sb_prompt_kernel,
        grid=(H_SB // SBP_NH, T // SBP_BQ),
        in_specs=[
            pl.BlockSpec(memory_space=pltpu.SMEM),
            pl.BlockSpec((SBP_BQ, width), lambda h, i: (i, q_col0 + h)),
            pl.BlockSpec((T, width), lambda h, i: (0, h)),
            pl.BlockSpec((T, width), lambda h, i: (0, h)),
        ],
        out_specs=pl.BlockSpec((SBP_BQ, width), lambda h, i: (i, h)),
        out_shape=jax.ShapeDtypeStruct((T, W_SB), BF16),
        scratch_shapes=[pltpu.VMEM((SBP_NH, SBP_BQ, DH_SB), F32), t_shape,
                        u_shape, u_shape, t_shape, t_shape],
        compiler_params=_params(("parallel", "arbitrary"), 52),
        name="sb_prompt",
    )(b_sb, p, kb, vb)


SBS_PPS = 16


def _later_aug(n):
    row = lax.broadcasted_iota(jnp.int32, (n, n + LANES), 0)
    col = lax.broadcasted_iota(jnp.int32, (n, n + LANES), 1)
    return ((row > col) | (col >= n)).astype(BF16)


def _sb_sample_kernel(pt_ref, bsb_ref, q_ref, kn_ref, vn_ref, *rest, n_q):
    page_refs = rest[:2 * SBS_PPS]
    o_ref, acc_ref, rest_ref = rest[2 * SBS_PPS:]
    g = pl.program_id(1)
    rows = H_SB * n_q

    qs = [(q_ref[:, h * DH_SB:(h + 1) * DH_SB] * SB_QSCALE).astype(BF16) for h in range(H_SB)]
    bias = jnp.concatenate([jnp.full((n_q, PAGE_SIZE), bsb_ref[h] * LOG2E, F32) for h in range(H_SB)],
                           axis=0)
    later_aug = _later_aug(PAGE_SIZE)

    def head_dots_nt(ks):
        return jnp.concatenate([_dot_nt(qs[h], ks[h]) for h in range(H_SB)], axis=0)

    def head_dots(ab, vs):
        return jnp.concatenate([_dot(ab[h * n_q:(h + 1) * n_q, :], vs[h]) for h in range(H_SB)], axis=0)

    @pl.when(g == 0)
    def _():
        pad = jnp.zeros((PAGE_SIZE - n_q, DH_SB), BF16)
        ks = [jnp.concatenate([kn_ref[:, h * DH_SB:(h + 1) * DH_SB].astype(BF16), pad], axis=0)
              for h in range(H_SB)]
        vs = [jnp.concatenate([vn_ref[:, h * DH_SB:(h + 1) * DH_SB].astype(BF16), pad], axis=0)
              for h in range(H_SB)]
        r = lax.broadcasted_iota(jnp.int32, (rows, PAGE_SIZE), 0)
        cidx = lax.broadcasted_iota(jnp.int32, (rows, PAGE_SIZE), 1)
        mask = cidx < (r % n_q)
        z = head_dots_nt(ks) + bias
        sp = _softplus2(z)
        ea = _dot(jnp.where(mask, sp, 0.0).astype(BF16), later_aug)
        a = jnp.where(mask, jnp.exp2(z - sp - ea[:, :PAGE_SIZE]), 0.0)
        acc_ref[...] = head_dots(a.astype(BF16), vs)
        rest_ref[...] = -ea[:, PAGE_SIZE:]

    def head_rows(refs, h):
        return jnp.concatenate([r[pl.ds(h, PAGE_SIZE, stride=H_SB), :] for r in refs],
                               axis=0).astype(BF16)

    ks = [head_rows(page_refs[:SBS_PPS], h) for h in range(H_SB)]
    vs = [head_rows(page_refs[SBS_PPS:], h) for h in range(H_SB)]
    z = head_dots_nt(ks) + jnp.concatenate([bias] * SBS_PPS, axis=1)
    sp = _softplus2(z)
    spb = sp.astype(BF16)
    rest_v = rest_ref[...]
    es, rests = [], []
    for c in range(SBS_PPS):
        ea = _dot(spb[:, c * PAGE_SIZE:(c + 1) * PAGE_SIZE], later_aug)
        es.append(ea[:, :PAGE_SIZE])
        rests.append(rest_v)
        rest_v = rest_v - ea[:, PAGE_SIZE:]
    a = jnp.exp2(z - sp - jnp.concatenate(es, axis=1) + jnp.concatenate(rests, axis=1))
    acc = acc_ref[...] + head_dots(a.astype(BF16), vs)
    acc_ref[...] = acc
    rest_ref[...] = rest_v

    @pl.when(g == pl.num_programs(1) - 1)
    def _():
        for h in range(H_SB):
            o_ref[:, h * DH_SB:(h + 1) * DH_SB] = acc[h * n_q:(h + 1) * n_q, :]


def _sb_sample(p, k_new, v_new, cache_k, cache_v, page_table, b_sb, *, n_q):
    n_seq, n_pages = page_table.shape
    n_groups = n_pages // SBS_PPS

    def page_spec(c):
        return pl.BlockSpec(
            (PAGE_SIZE * H_SB, DH_SB),
            lambda b, g, pt: (pt[b, n_pages - 1 - (g * SBS_PPS + c)], 0))

    return pl.pallas_call(
        functools.partial(_sb_sample_kernel, n_q=n_q),
        grid_spec=pltpu.PrefetchScalarGridSpec(
            num_scalar_prefetch=1,
            grid=(n_seq, n_groups),
            in_specs=[
                pl.BlockSpec(memory_space=pltpu.SMEM),
                pl.BlockSpec((n_q, W_SB), lambda b, g, pt: (b, 3)),
                pl.BlockSpec((n_q, W_SB), lambda b, g, pt: (b, 0)),
                pl.BlockSpec((n_q, W_SB), lambda b, g, pt: (b, 0)),
            ] + [page_spec(c) for c in range(SBS_PPS)] * 2,
            out_specs=pl.BlockSpec((n_q, W_SB), lambda b, g, pt: (b, 0)),
            scratch_shapes=[pltpu.VMEM((H_SB * n_q, DH_SB), F32),
                            pltpu.VMEM((H_SB * n_q, LANES), F32)],
        ),
        out_shape=jax.ShapeDtypeStruct((n_seq * n_q, W_SB), F32),
        compiler_params=_params(("parallel", "arbitrary"), 48),
        name="sb_sample",
    )(page_table, b_sb, p, k_new, v_new,
      *([cache_k] * SBS_PPS), *([cache_v] * SBS_PPS))


ROW_CHUNK = 256


def _layer_norm(x, g, b):
    mu = jnp.mean(x, axis=-1, keepdims=True)
    xc = x - mu
    var = jnp.mean(xc * xc, axis=-1, keepdims=True)
    return xc * lax.rsqrt(var + LN_EPS) * g + b


def _outproj_kernel(hm_ref, hs_ref, x_ref, wm_ref, ws_ref, g_ref, b_ref, x1_ref, x1b_ref, *, tm):
    for c in range(tm // ROW_CHUNK):
        rows = slice(c * ROW_CHUNK, (c + 1) * ROW_CHUNK)
        mix = (_dot(hm_ref[rows, :].astype(BF16), wm_ref[...])
               + _dot(hs_ref[rows, :].astype(BF16), ws_ref[...]))
        x1 = _layer_norm(DN_ALPHA * x_ref[rows, :] + mix, g_ref[...], b_ref[...])
        x1_ref[rows, :] = x1
        x1b_ref[rows, :] = x1.astype(BF16)


def _outproj(hm, hs, x, w_out, ln_g, ln_b, tm):
    m = x.shape[0]
    return pl.pallas_call(
        functools.partial(_outproj_kernel, tm=tm),
        grid=(m // tm,),
        in_specs=[
            pl.BlockSpec((tm, W_M), lambda i: (i, 0)),
            pl.BlockSpec((tm, W_SB), lambda i: (i, 0)),
            pl.BlockSpec((tm, D_MODEL), lambda i: (i, 0)),
            pl.BlockSpec((W_M, D_MODEL), lambda i: (0, 0)),
            pl.BlockSpec((W_SB, D_MODEL), lambda i: (1, 0)),
            pl.BlockSpec((1, D_MODEL), lambda i: (0, 0)),
            pl.BlockSpec((1, D_MODEL), lambda i: (0, 0)),
        ],
        out_specs=[
            pl.BlockSpec((tm, D_MODEL), lambda i: (i, 0)),
            pl.BlockSpec((tm, D_MODEL), lambda i: (i, 0)),
        ],
        out_shape=[
            jax.ShapeDtypeStruct((m, D_MODEL), F32),
            jax.ShapeDtypeStruct((m, D_MODEL), BF16),
        ],
        compiler_params=_params(("parallel",), 48),
        name="outproj_ln",
    )(hm, hs, x, w_out, w_out, ln_g, ln_b)


UP_TN = 512
UP_NJ = D_FF_PAD // UP_TN
UP_CHUNK = 512
CARRY = 8


def _gelu_tanh(x):
    return 0.5 * x * (1.0 + jnp.tanh(0.7978845608028654 * (x + 0.044715 * (x * x * x))))


def _conv_taps(buf_ref, h, off, wc_ref, bc_ref):
    n = h.shape[0]
    buf_ref[CARRY + off:CARRY + off + n, :] = h
    h1 = buf_ref[CARRY + off - 1:CARRY + off - 1 + n, :]
    h2 = buf_ref[CARRY + off - 2:CARRY + off - 2 + n, :]
    return bc_ref[...] + wc_ref[0:1, :] * h2 + wc_ref[1:2, :] * h1 + wc_ref[2:3, :] * h


def _ffn_up_prompt_kernel(x_ref, wu_ref, wg_ref, wcu_ref, wcg_ref, bcu_ref, bcg_ref, pu_ref, pg_ref,
                          act_ref, cu_ref, cg_ref, bu_s, bg_s, *, tm):
    i = pl.program_id(1)

    @pl.when(i == 0)
    def _():
        bu_s[CARRY - 2:CARRY, :] = pu_ref[...]
        bg_s[CARRY - 2:CARRY, :] = pg_ref[...]

    @pl.when(i > 0)
    def _():
        bu_s[0:CARRY, :] = bu_s[tm:tm + CARRY, :]
        bg_s[0:CARRY, :] = bg_s[tm:tm + CARRY, :]

    for c in range(tm // UP_CHUNK):
        off = c * UP_CHUNK
        x = x_ref[off:off + UP_CHUNK, :]
        hu = _dot(x, wu_ref[...])
        hg = _dot(x, wg_ref[...])
        cu = _conv_taps(bu_s, hu, off, wcu_ref, bcu_ref)
        cg = _conv_taps(bg_s, hg, off, wcg_ref, bcg_ref)
        act_ref[off:off + UP_CHUNK, :] = (_gelu_tanh(cg) * cu).astype(act_ref.dtype)
    cu_ref[...] = hu[UP_CHUNK - 2:UP_CHUNK, :]
    cg_ref[...] = hg[UP_CHUNK - 2:UP_CHUNK, :]


def _ffn_up_prompt(x1b, wu, wg, wcu, wcg, bcu, bcg, pu, pg, tm):
    m = x1b.shape[0]
    col = lambda j, i: (0, j)
    return pl.pallas_call(
        functools.partial(_ffn_up_prompt_kernel, tm=tm),
        grid=(UP_NJ, m // tm),
        in_specs=[
            pl.BlockSpec((tm, D_MODEL), lambda j, i: (i, 0)),
            pl.BlockSpec((D_MODEL, UP_TN), col),
            pl.BlockSpec((D_MODEL, UP_TN), col),
            pl.BlockSpec((CONV_W, UP_TN), col),
            pl.BlockSpec((CONV_W, UP_TN), col),
            pl.BlockSpec((1, UP_TN), col),
            pl.BlockSpec((1, UP_TN), col),
            pl.BlockSpec((CONV_W - 1, UP_TN), col),
            pl.BlockSpec((CONV_W - 1, UP_TN), col),
        ],
        out_specs=[
            pl.BlockSpec((tm, UP_TN), lambda j, i: (i, j)),
            pl.BlockSpec((CONV_W - 1, UP_TN), col),
            pl.BlockSpec((CONV_W - 1, UP_TN), col),
        ],
        out_shape=[
            jax.ShapeDtypeStruct((m, D_FF_PAD), BF16),
            jax.ShapeDtypeStruct((CONV_W - 1, D_FF_PAD), F32),
            jax.ShapeDtypeStruct((CONV_W - 1, D_FF_PAD), F32),
        ],
        scratch_shapes=[pltpu.VMEM((tm + CARRY, UP_TN), F32), pltpu.VMEM((tm + CARRY, UP_TN), F32)],
        compiler_params=_params(("parallel", "arbitrary"), 48),
        name="ffn_up_prompt",
    )(x1b, wu, wg, wcu, wcg, bcu, bcg, pu, pg)


def _conv_taps_seq(buf_ref, h, s0_ref, s1_ref, wc_ref, bc_ref, seg):
    n = h.shape[0]
    n_seq = n // seg
    h1s, h2s = [], []
    for s in range(h.shape[1] // LANES):
        lanes = slice(s * LANES, (s + 1) * LANES)
        buf_ref[s, CARRY:CARRY + n, :] = h[:, lanes]
        buf_ref[s, pl.ds(CARRY - 1, n_seq, stride=seg), :] = s1_ref[:, lanes]
        h1s.append(buf_ref[s, CARRY - 1:CARRY - 1 + n, :])
        buf_ref[s, pl.ds(CARRY - 2, n_seq, stride=seg), :] = s0_ref[:, lanes]
        h2s.append(buf_ref[s, CARRY - 2:CARRY - 2 + n, :])
    h1 = jnp.concatenate(h1s, axis=1)
    h2 = jnp.concatenate(h2s, axis=1)
    return bc_ref[...] + wc_ref[0:1, :] * h2 + wc_ref[1:2, :] * h1 + wc_ref[2:3, :] * h


def _ffn_up_sample_kernel(x_ref, wu_ref, wg_ref, wcu_ref, wcg_ref, bcu_ref, bcg_ref,
                          s0u_ref, s1u_ref, s0g_ref, s1g_ref,
                          act_ref, hu_ref, hg_ref, bu_s, bg_s, *, seg):
    x = x_ref[...]
    hu = _dot(x, wu_ref[...])
    hg = _dot(x, wg_ref[...])
    cu = _conv_taps_seq(bu_s, hu, s0u_ref, s1u_ref, wcu_ref, bcu_ref, seg)
    cg = _conv_taps_seq(bg_s, hg, s0g_ref, s1g_ref, wcg_ref, bcg_ref, seg)
    act_ref[...] = (_gelu_tanh(cg) * cu).astype(act_ref.dtype)
    hu_ref[...] = hu
    hg_ref[...] = hg


def _ffn_up_sample(x1b, wu, wg, wcu, wcg, bcu, bcg, s0u, s1u, s0g, s1g, seg):
    m = x1b.shape[0]
    n_seq = m // seg
    col = lambda j: (0, j)
    w_spec = pl.BlockSpec((D_MODEL, UP_TN), col)
    c_spec = pl.BlockSpec((CONV_W, UP_TN), col)
    b_spec = pl.BlockSpec((1, UP_TN), col)
    s_spec = pl.BlockSpec((n_seq, UP_TN), col)
    o_spec = pl.BlockSpec((m, UP_TN), col)
    return pl.pallas_call(
        functools.partial(_ffn_up_sample_kernel, seg=seg),
        grid=(UP_NJ,),
        in_specs=[pl.BlockSpec((m, D_MODEL), lambda j: (0, 0)), w_spec, w_spec, c_spec, c_spec,
                  b_spec, b_spec, s_spec, s_spec, s_spec, s_spec],
        out_specs=[o_spec, o_spec, o_spec],
        out_shape=[
            jax.ShapeDtypeStruct((m, D_FF_PAD), BF16),
            jax.ShapeDtypeStruct((m, D_FF_PAD), F32),
            jax.ShapeDtypeStruct((m, D_FF_PAD), F32),
        ],
        scratch_shapes=[pltpu.VMEM((UP_TN // LANES, m + CARRY, LANES), F32)] * 2,
        compiler_params=_params(("parallel",), 32),
        name="ffn_up_sample",
    )(x1b, wu, wg, wcu, wcg, bcu, bcg, s0u, s1u, s0g, s1g)


DOWN_TK = 1408
DOWN_NK = D_FF_PAD // DOWN_TK
DOWN_LAST = D_FF - (DOWN_NK - 1) * DOWN_TK


def _ffn_down_kernel(a_ref, w_ref, x1_ref, g_ref, b_ref, y_ref, acc_ref):
    k = pl.program_id(1)

    @pl.when(k == 0)
    def _():
        acc_ref[...] = _dot(a_ref[...], w_ref[...])

    @pl.when((k > 0) & (k < DOWN_NK - 1))
    def _():
        acc_ref[...] += _dot(a_ref[...], w_ref[...])

    @pl.when(k == DOWN_NK - 1)
    def _():
        ffn = acc_ref[...] + _dot(a_ref[:, :DOWN_LAST], w_ref[:DOWN_LAST, :])
        y_ref[...] = _layer_norm(DN_ALPHA * x1_ref[...] + ffn, g_ref[...], b_ref[...])


def _ffn_down(act, w_down, x1, ln_g, ln_b, tm):
    m = act.shape[0]
    return pl.pallas_call(
        _ffn_down_kernel,
        grid=(m // tm, DOWN_NK),
        in_specs=[
            pl.BlockSpec((tm, DOWN_TK), lambda i, k: (i, k)),
            pl.BlockSpec((DOWN_TK, D_MODEL), lambda i, k: (k, 0)),
            pl.BlockSpec((tm, D_MODEL), lambda i, k: (i, 0)),
            pl.BlockSpec((1, D_MODEL), lambda i, k: (0, 0)),
            pl.BlockSpec((1, D_MODEL), lambda i, k: (0, 0)),
        ],
        out_specs=pl.BlockSpec((tm, D_MODEL), lambda i, k: (i, 0)),
        out_shape=jax.ShapeDtypeStruct((m, D_MODEL), F32),
        scratch_shapes=[pltpu.VMEM((tm, D_MODEL), F32)],
        compiler_params=_params(("parallel", "arbitrary"), 48),
        name="ffn_down_ln",
    )(act, w_down, x1, ln_g, ln_b)


def _pad_cols(a, width):
    return jnp.pad(a, ((0, 0), (0, width - a.shape[1])))


def _halves(a, dtype=F32):
    return (_pad_cols(a[:, :D_FF], D_FF_PAD).astype(dtype), _pad_cols(a[:, D_FF:], D_FF_PAD).astype(dtype))


def _unsplit(u, g):
    return jnp.concatenate([u[..., :D_FF], g[..., :D_FF]], axis=-1)


def kernel(x_prompt, x_sample, cache_k, cache_v, state_C, state_n, state_m, state_conv, page_table,
           w_in, b_gates, b_sb, w_mlstm_norm, w_out, ln1_g, ln1_b, w_up, w_conv, b_conv, w_down,
           ln2_g, ln2_b):
    n_b, seq, _ = x_prompt.shape
    n_s, n_q, _ = x_sample.shape
    assert n_b == 1

    wi = w_in[0]
    w_a, w_b, w_gate = _prep_in(wi)
    bg = _pad_cols(b_gates[0][None, :], LANES)
    wn = w_mlstm_norm[0][None, :]
    wo = w_out[0].astype(BF16)
    wu, wg = _prep_up(w_up[0])
    wcu, wcg = _halves(w_conv[0])
    bcu, bcg = _halves(b_conv[0][None, :])
    wd = w_down[0].astype(BF16)
    g1, b1 = ln1_g[0][None, :], ln1_b[0][None, :]
    g2, b2 = ln2_g[0][None, :], ln2_b[0][None, :]
    bsb = b_sb[0]

    xp = x_prompt.reshape(seq, D_MODEL)
    p_p, k_p, v_p, kb_p, vb_p, gt_p = _inproj(xp, w_a, w_b, w_gate, 1024)
    hm_p, c_p, nn_p, mm_p = _mlstm(
        p_p, gt_p, bg, wn,
        jnp.zeros((1, H_M, DK_M, DV_M), F32), jnp.zeros((1, H_M, 1, DK_M), F32),
        jnp.zeros((1, H_M, 1, 1), F32), batch=1, L=256, out_dtype=BF16)
    hs_p = _sb_prompt(p_p, kb_p, vb_p, bsb)
    x1_p, x1b_p = _outproj(hm_p, hs_p, xp, wo, g1, b1, 512)
    conv0 = jnp.zeros((CONV_W - 1, D_FF_PAD), F32)
    act_p, cu_p, cg_p = _ffn_up_prompt(x1b_p, wu, wg, wcu, wcg, bcu, bcg, conv0, conv0, 2048)
    y_p = _ffn_down(act_p, wd, x1_p, g2, b2, 512)

    xs = x_sample.reshape(n_s * n_q, D_MODEL)
    p_s, k_s, v_s, _, _, gt_s = _inproj(xs, w_a, w_b, w_gate, n_s * n_q)
    hm_s, c_s, nn_s, mm_s = _mlstm(
        p_s, gt_s, bg, wn, state_C[0], state_n[0][:, :, None, :], state_m[0][:, :, None, None],
        batch=n_s, L=n_q, out_dtype=F32)
    hs_s = _sb_sample(p_s, k_s, v_s, cache_k.reshape(-1, DH_SB), cache_v.reshape(-1, DH_SB),
                      page_table, bsb, n_q=n_q)
    x1_s, x1b_s = _outproj(hm_s, hs_s, xs, wo, g1, b1, n_s * n_q)
    s0u, s0g = _halves(state_conv[0][:, 0, :])
    s1u, s1g = _halves(state_conv[0][:, 1, :])
    act_s, hu_s, hg_s = _ffn_up_sample(x1b_s, wu, wg, wcu, wcg, bcu, bcg, s0u, s1u, s0g, s1g, n_q)
    y_s = _ffn_down(act_s, wd, x1_s, g2, b2, n_s * n_q)

    conv_s = _unsplit(hu_s.reshape(n_s, n_q, D_FF_PAD)[:, n_q - 2:], hg_s.reshape(n_s, n_q, D_FF_PAD)[:, n_q - 2:])
    return (
        y_p.reshape(1, seq, D_MODEL),
        y_s.reshape(n_s, n_q, D_MODEL),
        k_p.reshape(1, 1, seq, H_SB, DH_SB),
        v_p.reshape(1, 1, seq, H_SB, DH_SB),
        c_p[None],
        nn_p.reshape(1, 1, H_M, DK_M),
        mm_p.reshape(1, 1, H_M),
        _unsplit(cu_p, cg_p)[None, None],
        k_s.reshape(1, n_s, n_q, H_SB, DH_SB),
        v_s.reshape(1, n_s, n_q, H_SB, DH_SB),
        c_s[None],
        nn_s.reshape(1, n_s, H_M, DK_M),
        mm_s.reshape(1, n_s, H_M),
        conv_s[None],
    )
```

```python
import functools

import jax
import jax.numpy as jnp
from jax import lax
from jax.experimental import pallas as pl
from jax.experimental.pallas import tpu as pltpu

F32 = jnp.float32
BF16 = jnp.bfloat16

D_MODEL = 2048
H_M = 4
DK_M = 128
DV_M = 256
W_M = H_M * DV_M
H_SB = 8
DH_SB = 128
W_SB = H_SB * DH_SB
D_FF = 5504
D_FF_PAD = 5632
CONV_W = 3
PAGE_SIZE = 128
LN_EPS = 1e-5
RMS_EPS = 1e-6
DN_ALPHA = 2.0 ** 0.25
GATE_COL = 3072
N_GATES = 2 * H_M
LANES = 128
MIB = 1024 * 1024
LOG2E = 1.4426950408889634


def _params(semantics, vmem_mib, flags=None):
    return pltpu.CompilerParams(dimension_semantics=semantics,
                                vmem_limit_bytes=int(vmem_mib * MIB), flags=flags)


def _dot(a, b):
    return jnp.dot(a, b, preferred_element_type=F32)


def _dot_nt(a, b):
    return lax.dot_general(a, b, (((1,), (1,)), ((), ())), preferred_element_type=F32)


def _dot_tn(a, b):
    return lax.dot_general(a, b, (((0,), (0,)), ((), ())), preferred_element_type=F32)


def _softplus(z):
    return jnp.maximum(z, 0.0) + jnp.log1p(jnp.exp(-jnp.abs(z)))


def _softplus2(z):
    return jnp.maximum(z, 0.0) + jnp.log2(1.0 + jnp.exp2(-jnp.abs(z)))


PREP_ROWS = 256


def _prep_up_kernel(u_ref, g_ref, wu_ref, wg_ref):
    pad = jnp.zeros((u_ref.shape[0], D_FF_PAD - D_FF), BF16)
    wu_ref[:, :D_FF] = u_ref[...].astype(BF16)
    wu_ref[:, D_FF:] = pad
    wg_ref[:, :D_FF] = g_ref[...].astype(BF16)
    wg_ref[:, D_FF:] = pad


def _prep_up(w_up):
    d = w_up.shape[0]
    return pl.pallas_call(
        _prep_up_kernel,
        grid=(d // PREP_ROWS,),
        in_specs=[pl.BlockSpec((PREP_ROWS, D_FF), lambda r: (r, 0)),
                  pl.BlockSpec((PREP_ROWS, D_FF), lambda r: (r, 1))],
        out_specs=[pl.BlockSpec((PREP_ROWS, D_FF_PAD), lambda r: (r, 0))] * 2,
        out_shape=[jax.ShapeDtypeStruct((d, D_FF_PAD), BF16)] * 2,
        compiler_params=_params(("parallel",), 40),
        name="prep_w_up",
    )(w_up, w_up)


IN_TN = 512
IN_NJ_A = GATE_COL // IN_TN
IN_NJ_P = 4096 // IN_TN
IN_NJ = IN_NJ_P + 4


def _inproj_kernel(x_ref, wa_ref, wb_ref, wg_ref, p_ref, k_ref, v_ref, kb_ref, vb_ref, g_ref, xb_ref):
    j = pl.program_id(1)

    @pl.when(j == 0)
    def _():
        xb = x_ref[...].astype(BF16)
        xb_ref[...] = xb
        g_ref[...] = _dot_nt(xb, wg_ref[...])

    @pl.when(j < IN_NJ_A)
    def _():
        p_ref[...] = _dot_nt(xb_ref[...], wa_ref[...])

    @pl.when(j >= IN_NJ_A)
    def _():
        acc = _dot_nt(xb_ref[...], wb_ref[...])

        @pl.when(j < IN_NJ_P)
        def _():
            p_ref[...] = acc

        @pl.when((j >= IN_NJ_P) & (j < IN_NJ_P + 2))
        def _():
            k_ref[...] = acc
            kb_ref[...] = acc.astype(BF16)

        @pl.when(j >= IN_NJ_P + 2)
        def _():
            v_ref[...] = acc
            vb_ref[...] = acc.astype(BF16)


def _inproj(x, w_a, w_b, w_gate, tm):
    m = x.shape[0]
    k_blk = lambda i, j: (i, jnp.clip(j - IN_NJ_P, 0, 1))
    v_blk = lambda i, j: (i, jnp.clip(j - IN_NJ_P - 2, 0, 1))
    return pl.pallas_call(
        _inproj_kernel,
        grid=(m // tm, IN_NJ),
        in_specs=[
            pl.BlockSpec((tm, D_MODEL), lambda i, j: (i, 0)),
            pl.BlockSpec((IN_TN, D_MODEL), lambda i, j: (jnp.minimum(j, IN_NJ_A - 1), 0)),
            pl.BlockSpec((IN_TN, D_MODEL), lambda i, j: (jnp.maximum(j - IN_NJ_A, 0), 0)),
            pl.BlockSpec((LANES, D_MODEL), lambda i, j: (0, 0)),
        ],
        out_specs=[
            pl.BlockSpec((tm, IN_TN), lambda i, j: (i, jnp.minimum(j, IN_NJ_P - 1))),
            pl.BlockSpec((tm, IN_TN), k_blk),
            pl.BlockSpec((tm, IN_TN), v_blk),
            pl.BlockSpec((tm, IN_TN), k_blk),
            pl.BlockSpec((tm, IN_TN), v_blk),
            pl.BlockSpec((tm, LANES), lambda i, j: (i, 0)),
        ],
        out_shape=[
            jax.ShapeDtypeStruct((m, 4096), F32),
            jax.ShapeDtypeStruct((m, W_SB), F32),
            jax.ShapeDtypeStruct((m, W_SB), F32),
            jax.ShapeDtypeStruct((m, W_SB), BF16),
            jax.ShapeDtypeStruct((m, W_SB), BF16),
            jax.ShapeDtypeStruct((m, LANES), F32),
        ],
        scratch_shapes=[pltpu.VMEM((tm, D_MODEL), BF16)],
        compiler_params=_params(("parallel", "arbitrary"), 54),
        name="inproj",
    )(x, w_a, w_b, w_gate)


def _mlstm_kernel(q_ref, k_ref, v_ref, o_ref, g_ref, bg_ref, wn_ref, c0_ref, n0_ref, m0_ref,
                  h_ref, cout_ref, nout_ref, mout_ref, c_s, n_s, m_s, *, L):
    c = pl.program_id(1)

    @pl.when(c == 0)
    def _():
        c_s[...] = c0_ref[0]
        n_s[...] = n0_ref[0]
        m_s[...] = m0_ref[0]

    g = g_ref[...] + bg_ref[...]
    row = lax.broadcasted_iota(jnp.int32, (L, L), 0)
    col = lax.broadcasted_iota(jnp.int32, (L, L), 1)
    causal = col <= row
    eye = col == row
    for h in range(H_M):
        li_col = g[:, h:h + 1]
        fg_col = g[:, H_M + h:H_M + h + 1]
        lf_col = -_softplus(-fg_col)
        li_row = jnp.sum(jnp.where(eye, li_col, 0.0), axis=0, keepdims=True)
        lf_row = jnp.sum(jnp.where(eye, lf_col, 0.0), axis=0, keepdims=True)
        b_row = jnp.sum(jnp.where(row <= col, lf_col, 0.0), axis=0, keepdims=True)
        b_col = jnp.sum(jnp.where(causal, lf_row, 0.0), axis=1, keepdims=True)
        b_last = b_col[L - 1:L, :]
        m_prev = m_s[h]
        d = jnp.where(causal, b_col - b_row + li_row, -jnp.inf)
        a_col = b_col + m_prev
        m_t = jnp.maximum(a_col, jnp.max(d, axis=1, keepdims=True))
        w_intra = jnp.exp(d - m_t)
        w_inter = jnp.exp(a_col - m_t)

        qf = q_ref[:, h * DK_M:(h + 1) * DK_M] * (DK_M ** -0.5)
        kf = k_ref[:, h * DK_M:(h + 1) * DK_M]
        qb = qf.astype(BF16)
        kb = kf.astype(BF16)
        vb = v_ref[:, h * DV_M:(h + 1) * DV_M].astype(BF16)
        c_prev = c_s[h]
        n_prev = n_s[h]

        s = _dot_nt(qb, kb) * w_intra
        num = w_inter * _dot(qb, c_prev.astype(BF16)) + _dot(s.astype(BF16), vb)
        den = (w_inter * jnp.sum(qf * n_prev, axis=1, keepdims=True)
               + jnp.sum(s, axis=1, keepdims=True))
        hh = num / jnp.maximum(jnp.abs(den), jnp.exp(-m_t))

        m_new = m_t[L - 1:L, :]
        w_end = jnp.exp(b_last - b_col + li_col - m_new)
        decay = jnp.exp(b_last + m_prev - m_new)
        kw = kf * w_end
        c_s[h] = decay * c_prev + _dot_tn(kw.astype(BF16), vb)
        n_s[h] = decay * n_prev + jnp.sum(kw, axis=0, keepdims=True)
        m_s[h] = m_new

        hn = hh * lax.rsqrt(jnp.mean(hh * hh, axis=1, keepdims=True) + RMS_EPS)
        hn = hn * wn_ref[:, h * DV_M:(h + 1) * DV_M]
        og = jax.nn.sigmoid(o_ref[:, h * DV_M:(h + 1) * DV_M])
        h_ref[:, h * DV_M:(h + 1) * DV_M] = (og * hn).astype(h_ref.dtype)

    @pl.when(c == pl.num_programs(1) - 1)
    def _():
        cout_ref[0] = c_s[...]
        nout_ref[0] = n_s[...]
        mout_ref[0] = m_s[...]


def _mlstm(p, gates, bg, wn, c0, n0, m0, *, batch, L, out_dtype):
    t_total = p.shape[0]
    nc = t_total // (batch * L)
    row = lambda b, c: b * nc + c
    return pl.pallas_call(
        functools.partial(_mlstm_kernel, L=L),
        grid=(batch, nc),
        in_specs=[
            pl.BlockSpec((L, H_M * DK_M), lambda b, c: (row(b, c), 0)),
            pl.BlockSpec((L, H_M * DK_M), lambda b, c: (row(b, c), 1)),
            pl.BlockSpec((L, W_M), lambda b, c: (row(b, c), 1)),
            pl.BlockSpec((L, W_M), lambda b, c: (row(b, c), 2)),
            pl.BlockSpec((L, LANES), lambda b, c: (row(b, c), 0)),
            pl.BlockSpec((1, LANES), lambda b, c: (0, 0)),
            pl.BlockSpec((1, W_M), lambda b, c: (0, 0)),
            pl.BlockSpec((1, H_M, DK_M, DV_M), lambda b, c: (b, 0, 0, 0)),
            pl.BlockSpec((1, H_M, 1, DK_M), lambda b, c: (b, 0, 0, 0)),
            pl.BlockSpec((1, H_M, 1, 1), lambda b, c: (b, 0, 0, 0)),
        ],
        out_specs=[
            pl.BlockSpec((L, W_M), lambda b, c: (row(b, c), 0)),
            pl.BlockSpec((1, H_M, DK_M, DV_M), lambda b, c: (b, 0, 0, 0)),
            pl.BlockSpec((1, H_M, 1, DK_M), lambda b, c: (b, 0, 0, 0)),
            pl.BlockSpec((1, H_M, 1, 1), lambda b, c: (b, 0, 0, 0)),
        ],
        out_shape=[
            jax.ShapeDtypeStruct((t_total, W_M), out_dtype),
            jax.ShapeDtypeStruct((batch, H_M, DK_M, DV_M), F32),
            jax.ShapeDtypeStruct((batch, H_M, 1, DK_M), F32),
            jax.ShapeDtypeStruct((batch, H_M, 1, 1), F32),
        ],
        scratch_shapes=[
            pltpu.VMEM((H_M, DK_M, DV_M), F32),
            pltpu.VMEM((H_M, 1, DK_M), F32),
            pltpu.VMEM((H_M, 1, 1), F32),
        ],
        compiler_params=_params(("parallel", "arbitrary"), 32),
        name="mlstm",
    )(p, p, p, p, gates, bg, wn, c0, n0, m0)


SBP_BQ = 512
SBP_BK = 256
SBP_NH = 4
SB_QSCALE = (DH_SB ** -0.5) * LOG2E


def _sb_prompt_kernel(bsb_ref, q_ref, k_ref, v_ref, o_ref, acc_ref, rest_ref,
                      u0_ref, u1_ref, t0_ref, t1_ref):
    hp = pl.program_id(0)
    i = pl.program_id(1)
    bq, bk = SBP_BQ, SBP_BK
    per_q = bq // bk
    nk = per_q * (i + 1)
    heads = [slice(hh * DH_SB, (hh + 1) * DH_SB) for hh in range(SBP_NH)]
    qs = [(q_ref[:, hs] * SB_QSCALE).astype(BF16) for hs in heads]
    bias = [bsb_ref[hp * SBP_NH + hh] * LOG2E for hh in range(SBP_NH)]
    later_aug = _later_aug(bk)
    row = lax.broadcasted_iota(jnp.int32, (bq, bk), 0)
    col = lax.broadcasted_iota(jnp.int32, (bq, bk), 1)
    acc_ref[...] = jnp.zeros_like(acc_ref)
    rest_ref[...] = jnp.zeros_like(rest_ref)

    def key_rows(jj):
        return pl.ds(pl.multiple_of((nk - 1 - jj) * bk, bk), bk)

    def scores(jj, u_ref, t_ref, diag):
        rows = key_rows(jj)
        mask = None if diag is None else (col + bk * (per_q - 1 - diag)) < row
        for hh, hs in enumerate(heads):
            z = _dot_nt(qs[hh], k_ref[rows, hs]) + bias[hh]
            sp = _softplus2(z)
            spm = sp if mask is None else jnp.where(mask, sp, 0.0)
            ea = _dot(spm.astype(BF16), later_aug)
            u = z - sp - ea[:, :bk]
            u_ref[hh] = u if mask is None else jnp.where(mask, u, -jnp.inf)
            t_ref[hh] = ea[:, bk:]

    def values(jj, u_ref, t_ref):
        rows = key_rows(jj)
        for hh, hs in enumerate(heads):
            rest = rest_ref[hh]
            a = jnp.exp2(u_ref[hh] + jnp.concatenate([rest] * (bk // LANES), axis=1))
            acc_ref[hh] += _dot(a.astype(BF16), v_ref[rows, hs])
            rest_ref[hh] = rest - t_ref[hh]

    scores(0, u0_ref, t0_ref, 0)
    scores(1, u1_ref, t1_ref, 1)
    values(0, u0_ref, t0_ref)

    def pair(p, carry):
        jj = per_q * (p + 1)
        scores(jj, u0_ref, t0_ref, None)
        values(jj - 1, u1_ref, t1_ref)
        scores(jj + 1, u1_ref, t1_ref, None)
        values(jj, u0_ref, t0_ref)
        return carry

    lax.fori_loop(0, i, pair, 0)
    values(nk - 1, u1_ref, t1_ref)
    for hh, hs in enumerate(heads):
        o_ref[:, hs] = acc_ref[hh].astype(o_ref.dtype)


def _sb_prompt(p, kb, vb, b_sb):
    assert SBP_BQ == 2 * SBP_BK
    T = p.shape[0]
    width = SBP_NH * DH_SB
    q_col0 = 3072 // width
    u_shape = pltpu.VMEM((SBP_NH, SBP_BQ, SBP_BK), F32)
    t_shape = pltpu.VMEM((SBP_NH, SBP_BQ, LANES), F32)
    return pl.pallas_call(
        _sb_prompt_kernel,
        grid=(H_SB // SBP_NH, T // SBP_BQ),
        in_specs=[
            pl.BlockSpec(memory_space=pltpu.SMEM),
            pl.BlockSpec((SBP_BQ, width), lambda h, i: (i, q_col0 + h)),
            pl.BlockSpec((T, width), lambda h, i: (0, h)),
            pl.BlockSpec((T, width), lambda h, i: (0, h)),
        ],
        out_specs=pl.BlockSpec((SBP_BQ, width), lambda h, i: (i, h)),
        out_shape=jax.ShapeDtypeStruct((T, W_SB), BF16),
        scratch_shapes=[pltpu.VMEM((SBP_NH, SBP_BQ, DH_SB), F32), t_shape,
                        u_shape, u_shape, t_shape, t_shape],
        compiler_params=_params(("parallel", "arbitrary"), 52),
        name="sb_prompt",
    )(b_sb, p, kb, vb)


SBS_PPS = 16


def _later_aug(n):
    row = lax.broadcasted_iota(jnp.int32, (n, n + LANES), 0)
    col = lax.broadcasted_iota(jnp.int32, (n, n + LANES), 1)
    return ((row > col) | (col >= n)).astype(BF16)


def _sb_sample_kernel(pt_ref, bsb_ref, q_ref, kn_ref, vn_ref, *rest, n_q):
    page_refs = rest[:2 * SBS_PPS]
    o_ref, acc_ref, rest_ref = rest[2 * SBS_PPS:]
    g = pl.program_id(1)
    rows = H_SB * n_q

    qs = [(q_ref[:, h * DH_SB:(h + 1) * DH_SB] * SB_QSCALE).astype(BF16) for h in range(H_SB)]
    bias = jnp.concatenate([jnp.full((n_q, PAGE_SIZE), bsb_ref[h] * LOG2E, F32) for h in range(H_SB)],
                           axis=0)
    later_aug = _later_aug(PAGE_SIZE)

    def head_dots_nt(ks):
        return jnp.concatenate([_dot_nt(qs[h], ks[h]) for h in range(H_SB)], axis=0)

    def head_dots(ab, vs):
        return jnp.concatenate([_dot(ab[h * n_q:(h + 1) * n_q, :], vs[h]) for h in range(H_SB)], axis=0)

    @pl.when(g == 0)
    def _():
        pad = jnp.zeros((PAGE_SIZE - n_q, DH_SB), BF16)
        ks = [jnp.concatenate([kn_ref[:, h * DH_SB:(h + 1) * DH_SB].astype(BF16), pad], axis=0)
              for h in range(H_SB)]
        vs = [jnp.concatenate([vn_ref[:, h * DH_SB:(h + 1) * DH_SB].astype(BF16), pad], axis=0)
              for h in range(H_SB)]
        r = lax.broadcasted_iota(jnp.int32, (rows, PAGE_SIZE), 0)
        cidx = lax.broadcasted_iota(jnp.int32, (rows, PAGE_SIZE), 1)
        mask = cidx < (r % n_q)
        z = head_dots_nt(ks) + bias
        sp = _softplus2(z)
        ea = _dot(jnp.where(mask, sp, 0.0).astype(BF16), later_aug)
        a = jnp.where(mask, jnp.exp2(z - sp - ea[:, :PAGE_SIZE]), 0.0)
        acc_ref[...] = head_dots(a.astype(BF16), vs)
        rest_ref[...] = -ea[:, PAGE_SIZE:]

    def head_rows(refs, h):
        return jnp.concatenate([r[pl.ds(h, PAGE_SIZE, stride=H_SB), :] for r in refs],
                               axis=0).astype(BF16)

    ks = [head_rows(page_refs[:SBS_PPS], h) for h in range(H_SB)]
    vs = [head_rows(page_refs[SBS_PPS:], h) for h in range(H_SB)]
    z = head_dots_nt(ks) + jnp.concatenate([bias] * SBS_PPS, axis=1)
    sp = _softplus2(z)
    spb = sp.astype(BF16)
    rest_v = rest_ref[...]
    es, rests = [], []
    for c in range(SBS_PPS):
        ea = _dot(spb[:, c * PAGE_SIZE:(c + 1) * PAGE_SIZE], later_aug)
        es.append(ea[:, :PAGE_SIZE])
        rests.append(rest_v)
        rest_v = rest_v - ea[:, PAGE_SIZE:]
    a = jnp.exp2(z - sp - jnp.concatenate(es, axis=1) + jnp.concatenate(rests, axis=1))
    acc = acc_ref[...] + head_dots(a.astype(BF16), vs)
    acc_ref[...] = acc
    rest_ref[...] = rest_v

    @pl.when(g == pl.num_programs(1) - 1)
    def _():
        for h in range(H_SB):
            o_ref[:, h * DH_SB:(h + 1) * DH_SB] = acc[h * n_q:(h + 1) * n_q, :]


def _sb_sample(p, k_new, v_new, cache_k, cache_v, page_table, b_sb, *, n_q):
    n_seq, n_pages = page_table.shape
    n_groups = n_pages // SBS_PPS

    def page_spec(c):
        return pl.BlockSpec(
            (PAGE_SIZE * H_SB, DH_SB),
            lambda b, g, pt: (pt[b, n_pages - 1 - (g * SBS_PPS + c)], 0))

    return pl.pallas_call(
        functools.partial(_sb_sample_kernel, n_q=n_q),
        grid_spec=pltpu.PrefetchScalarGridSpec(
            num_scalar_prefetch=1,
            grid=(n_seq, n_groups),
            in_specs=[
                pl.BlockSpec(memory_space=pltpu.SMEM),
                pl.BlockSpec((n_q, W_SB), lambda b, g, pt: (b, 3)),
                pl.BlockSpec((n_q, W_SB), lambda b, g, pt: (b, 0)),
                pl.BlockSpec((n_q, W_SB), lambda b, g, pt: (b, 0)),
            ] + [page_spec(c) for c in range(SBS_PPS)] * 2,
            out_specs=pl.BlockSpec((n_q, W_SB), lambda b, g, pt: (b, 0)),
            scratch_shapes=[pltpu.VMEM((H_SB * n_q, DH_SB), F32),
                            pltpu.VMEM((H_SB * n_q, LANES), F32)],
        ),
        out_shape=jax.ShapeDtypeStruct((n_seq * n_q, W_SB), F32),
        compiler_params=_params(("parallel", "arbitrary"), 48),
        name="sb_sample",
    )(page_table, b_sb, p, k_new, v_new,
      *([cache_k] * SBS_PPS), *([cache_v] * SBS_PPS))


ROW_CHUNK = 256


def _layer_norm(x, g, b):
    mu = jnp.mean(x, axis=-1, keepdims=True)
    xc = x - mu
    var = jnp.mean(xc * xc, axis=-1, keepdims=True)
    return xc * lax.rsqrt(var + LN_EPS) * g + b


def _outproj_kernel(hm_ref, hs_ref, x_ref, wm_ref, ws_ref, g_ref, b_ref, x1_ref, x1b_ref, *, tm):
    for c in range(tm // ROW_CHUNK):
        rows = slice(c * ROW_CHUNK, (c + 1) * ROW_CHUNK)
        mix = (_dot(hm_ref[rows, :].astype(BF16), wm_ref[...])
               + _dot(hs_ref[rows, :].astype(BF16), ws_ref[...]))
        x1 = _layer_norm(DN_ALPHA * x_ref[rows, :] + mix, g_ref[...], b_ref[...])
        x1_ref[rows, :] = x1
        x1b_ref[rows, :] = x1.astype(BF16)


def _outproj(hm, hs, x, w_out, ln_g, ln_b, tm):
    m = x.shape[0]
    return pl.pallas_call(
        functools.partial(_outproj_kernel, tm=tm),
        grid=(m // tm,),
        in_specs=[
            pl.BlockSpec((tm, W_M), lambda i: (i, 0)),
            pl.BlockSpec((tm, W_SB), lambda i: (i, 0)),
            pl.BlockSpec((tm, D_MODEL), lambda i: (i, 0)),
            pl.BlockSpec((W_M, D_MODEL), lambda i: (0, 0)),
            pl.BlockSpec((W_SB, D_MODEL), lambda i: (1, 0)),
            pl.BlockSpec((1, D_MODEL), lambda i: (0, 0)),
            pl.BlockSpec((1, D_MODEL), lambda i: (0, 0)),
        ],
        out_specs=[
            pl.BlockSpec((tm, D_MODEL), lambda i: (i, 0)),
            pl.BlockSpec((tm, D_MODEL), lambda i: (i, 0)),
        ],
        out_shape=[
            jax.ShapeDtypeStruct((m, D_MODEL), F32),
            jax.ShapeDtypeStruct((m, D_MODEL), BF16),
        ],
        compiler_params=_params(("parallel",), 48),
        name="outproj_ln",
    )(hm, hs, x, w_out, w_out, ln_g, ln_b)


UP_TN = 512
UP_NJ = D_FF_PAD // UP_TN
UP_CHUNK = 512
CARRY = 8


GELU_C1 = 0.7978845608028654
GELU_C3 = GELU_C1 * 0.044715


def _gelu_tanh(x):
    half = 0.5 * x
    return half + half * jnp.tanh(x * (GELU_C1 + GELU_C3 * (x * x)))


def _conv_taps(buf_ref, h, off, wc_ref, bc_ref):
    n = h.shape[0]
    buf_ref[CARRY + off:CARRY + off + n, :] = h
    h1 = buf_ref[CARRY + off - 1:CARRY + off - 1 + n, :]
    h2 = buf_ref[CARRY + off - 2:CARRY + off - 2 + n, :]
    return bc_ref[...] + wc_ref[0:1, :] * h2 + wc_ref[1:2, :] * h1 + wc_ref[2:3, :] * h


def _ffn_up_prompt_kernel(x_ref, wu_ref, wg_ref, wcu_ref, wcg_ref, bcu_ref, bcg_ref, pu_ref, pg_ref,
                          act_ref, cu_ref, cg_ref, bu_s, bg_s, *, tm):
    i = pl.program_id(1)

    @pl.when(i == 0)
    def _():
        bu_s[CARRY - 2:CARRY, :] = pu_ref[...]
        bg_s[CARRY - 2:CARRY, :] = pg_ref[...]

    @pl.when(i > 0)
    def _():
        bu_s[0:CARRY, :] = bu_s[tm:tm + CARRY, :]
        bg_s[0:CARRY, :] = bg_s[tm:tm + CARRY, :]

    for c in range(tm // UP_CHUNK):
        off = c * UP_CHUNK
        x = x_ref[off:off + UP_CHUNK, :]
        hu = _dot(x, wu_ref[...])
        hg = _dot(x, wg_ref[...])
        cu = _conv_taps(bu_s, hu, off, wcu_ref, bcu_ref)
        cg = _conv_taps(bg_s, hg, off, wcg_ref, bcg_ref)
        act_ref[off:off + UP_CHUNK, :] = (_gelu_tanh(cg) * cu).astype(act_ref.dtype)
    cu_ref[...] = hu[UP_CHUNK - 2:UP_CHUNK, :]
    cg_ref[...] = hg[UP_CHUNK - 2:UP_CHUNK, :]


def _ffn_up_prompt(x1b, wu, wg, wcu, wcg, bcu, bcg, pu, pg, tm):
    m = x1b.shape[0]
    col = lambda j, i: (0, j)
    return pl.pallas_call(
        functools.partial(_ffn_up_prompt_kernel, tm=tm),
        grid=(UP_NJ, m // tm),
        in_specs=[
            pl.BlockSpec((tm, D_MODEL), lambda j, i: (i, 0)),
            pl.BlockSpec((D_MODEL, UP_TN), col),
            pl.BlockSpec((D_MODEL, UP_TN), col),
            pl.BlockSpec((CONV_W, UP_TN), col),
            pl.BlockSpec((CONV_W, UP_TN), col),
            pl.BlockSpec((1, UP_TN), col),
            pl.BlockSpec((1, UP_TN), col),
            pl.BlockSpec((CONV_W - 1, UP_TN), col),
            pl.BlockSpec((CONV_W - 1, UP_TN), col),
        ],
        out_specs=[
            pl.BlockSpec((tm, UP_TN), lambda j, i: (i, j)),
            pl.BlockSpec((CONV_W - 1, UP_TN), col),
            pl.BlockSpec((CONV_W - 1, UP_TN), col),
        ],
        out_shape=[
            jax.ShapeDtypeStruct((m, D_FF_PAD), BF16),
            jax.ShapeDtypeStruct((CONV_W - 1, D_FF_PAD), F32),
            jax.ShapeDtypeStruct((CONV_W - 1, D_FF_PAD), F32),
        ],
        scratch_shapes=[pltpu.VMEM((tm + CARRY, UP_TN), F32), pltpu.VMEM((tm + CARRY, UP_TN), F32)],
        compiler_params=_params(("parallel", "arbitrary"), 48),
        name="ffn_up_prompt",
    )(x1b, wu, wg, wcu, wcg, bcu, bcg, pu, pg)


def _conv_taps_seq(buf_ref, h, s0_ref, s1_ref, wc_ref, bc_ref, seg):
    n = h.shape[0]
    n_seq = n // seg
    h1s, h2s = [], []
    for s in range(h.shape[1] // LANES):
        lanes = slice(s * LANES, (s + 1) * LANES)
        buf_ref[s, CARRY:CARRY + n, :] = h[:, lanes]
        buf_ref[s, pl.ds(CARRY - 1, n_seq, stride=seg), :] = s1_ref[:, lanes]
        h1s.append(buf_ref[s, CARRY - 1:CARRY - 1 + n, :])
        buf_ref[s, pl.ds(CARRY - 2, n_seq, stride=seg), :] = s0_ref[:, lanes]
        h2s.append(buf_ref[s, CARRY - 2:CARRY - 2 + n, :])
    h1 = jnp.concatenate(h1s, axis=1)
    h2 = jnp.concatenate(h2s, axis=1)
    return bc_ref[...] + wc_ref[0:1, :] * h2 + wc_ref[1:2, :] * h1 + wc_ref[2:3, :] * h


def _ffn_up_sample_kernel(x_ref, wu_ref, wg_ref, wcu_ref, wcg_ref, bcu_ref, bcg_ref,
                          s0u_ref, s1u_ref, s0g_ref, s1g_ref,
                          act_ref, hu_ref, hg_ref, bu_s, bg_s, *, seg):
    x = x_ref[...]
    hu = _dot(x, wu_ref[...])
    hg = _dot(x, wg_ref[...])
    cu = _conv_taps_seq(bu_s, hu, s0u_ref, s1u_ref, wcu_ref, bcu_ref, seg)
    cg = _conv_taps_seq(bg_s, hg, s0g_ref, s1g_ref, wcg_ref, bcg_ref, seg)
    act_ref[...] = (_gelu_tanh(cg) * cu).astype(act_ref.dtype)
    hu_ref[...] = hu
    hg_ref[...] = hg


def _ffn_up_sample(x1b, wu, wg, wcu, wcg, bcu, bcg, s0u, s1u, s0g, s1g, seg):
    m = x1b.shape[0]
    n_seq = m // seg
    col = lambda j: (0, j)
    w_spec = pl.BlockSpec((D_MODEL, UP_TN), col)
    c_spec = pl.BlockSpec((CONV_W, UP_TN), col)
    b_spec = pl.BlockSpec((1, UP_TN), col)
    s_spec = pl.BlockSpec((n_seq, UP_TN), col)
    o_spec = pl.BlockSpec((m, UP_TN), col)
    return pl.pallas_call(
        functools.partial(_ffn_up_sample_kernel, seg=seg),
        grid=(UP_NJ,),
        in_specs=[pl.BlockSpec((m, D_MODEL), lambda j: (0, 0)), w_spec, w_spec, c_spec, c_spec,
                  b_spec, b_spec, s_spec, s_spec, s_spec, s_spec],
        out_specs=[o_spec, o_spec, o_spec],
        out_shape=[
            jax.ShapeDtypeStruct((m, D_FF_PAD), BF16),
            jax.ShapeDtypeStruct((m, D_FF_PAD), F32),
            jax.ShapeDtypeStruct((m, D_FF_PAD), F32),
        ],
        scratch_shapes=[pltpu.VMEM((UP_TN // LANES, m + CARRY, LANES), F32)] * 2,
        compiler_params=_params(("parallel",), 32),
        name="ffn_up_sample",
    )(x1b, wu, wg, wcu, wcg, bcu, bcg, s0u, s1u, s0g, s1g)


DOWN_TK = 1408
DOWN_NK = D_FF_PAD // DOWN_TK
DOWN_LAST = D_FF - (DOWN_NK - 1) * DOWN_TK


def _ffn_down_kernel(a_ref, w_ref, x1_ref, g_ref, b_ref, y_ref, acc_ref):
    k = pl.program_id(1)

    @pl.when(k == 0)
    def _():
        acc_ref[...] = _dot(a_ref[...], w_ref[...])

    @pl.when((k > 0) & (k < DOWN_NK - 1))
    def _():
        acc_ref[...] += _dot(a_ref[...], w_ref[...])

    @pl.when(k == DOWN_NK - 1)
    def _():
        ffn = acc_ref[...] + _dot(a_ref[:, :DOWN_LAST], w_ref[:DOWN_LAST, :])
        y_ref[...] = _layer_norm(DN_ALPHA * x1_ref[...] + ffn, g_ref[...], b_ref[...])


def _ffn_down(act, w_down, x1, ln_g, ln_b, tm):
    m = act.shape[0]
    return pl.pallas_call(
        _ffn_down_kernel,
        grid=(m // tm, DOWN_NK),
        in_specs=[
            pl.BlockSpec((tm, DOWN_TK), lambda i, k: (i, k)),
            pl.BlockSpec((DOWN_TK, D_MODEL), lambda i, k: (k, 0)),
            pl.BlockSpec((tm, D_MODEL), lambda i, k: (i, 0)),
            pl.BlockSpec((1, D_MODEL), lambda i, k: (0, 0)),
            pl.BlockSpec((1, D_MODEL), lambda i, k: (0, 0)),
        ],
        out_specs=pl.BlockSpec((tm, D_MODEL), lambda i, k: (i, 0)),
        out_shape=jax.ShapeDtypeStruct((m, D_MODEL), F32),
        scratch_shapes=[pltpu.VMEM((tm, D_MODEL), F32)],
        compiler_params=_params(("parallel", "arbitrary"), 48),
        name="ffn_down_ln",
    )(act, w_down, x1, ln_g, ln_b)


def _pad_cols(a, width):
    return jnp.pad(a, ((0, 0), (0, width - a.shape[1])))


def _halves(a, dtype=F32):
    return (_pad_cols(a[:, :D_FF], D_FF_PAD).astype(dtype), _pad_cols(a[:, D_FF:], D_FF_PAD).astype(dtype))


def _unsplit(u, g):
    return jnp.concatenate([u[..., :D_FF], g[..., :D_FF]], axis=-1)


def kernel(x_prompt, x_sample, cache_k, cache_v, state_C, state_n, state_m, state_conv, page_table,
           w_in, b_gates, b_sb, w_mlstm_norm, w_out, ln1_g, ln1_b, w_up, w_conv, b_conv, w_down,
           ln2_g, ln2_b):
    n_b, seq, _ = x_prompt.shape
    n_s, n_q, _ = x_sample.shape
    assert n_b == 1

    wi = w_in[0]
    wi_t = wi.T
    w_a = wi_t[:GATE_COL].astype(BF16)
    w_b = wi_t[GATE_COL + N_GATES:].astype(BF16)
    w_gate = jnp.pad(wi_t[GATE_COL:GATE_COL + N_GATES], ((0, LANES - N_GATES), (0, 0))).astype(BF16)
    bg = _pad_cols(b_gates[0][None, :], LANES)
    wn = w_mlstm_norm[0][None, :]
    wo = w_out[0].astype(BF16)
    wu, wg = _prep_up(w_up[0])
    wcu, wcg = _halves(w_conv[0])
    bcu, bcg = _halves(b_conv[0][None, :])
    wd = w_down[0].astype(BF16)
    g1, b1 = ln1_g[0][None, :], ln1_b[0][None, :]
    g2, b2 = ln2_g[0][None, :], ln2_b[0][None, :]
    bsb = b_sb[0]

    xp = x_prompt.reshape(seq, D_MODEL)
    p_p, k_p, v_p, kb_p, vb_p, gt_p = _inproj(xp, w_a, w_b, w_gate, 1024)
    hm_p, c_p, nn_p, mm_p = _mlstm(
        p_p, gt_p, bg, wn,
        jnp.zeros((1, H_M, DK_M, DV_M), F32), jnp.zeros((1, H_M, 1, DK_M), F32),
        jnp.zeros((1, H_M, 1, 1), F32), batch=1, L=256, out_dtype=BF16)
    hs_p = _sb_prompt(p_p, kb_p, vb_p, bsb)
    x1_p, x1b_p = _outproj(hm_p, hs_p, xp, wo, g1, b1, 512)
    conv0 = jnp.zeros((CONV_W - 1, D_FF_PAD), F32)
    act_p, cu_p, cg_p = _ffn_up_prompt(x1b_p, wu, wg, wcu, wcg, bcu, bcg, conv0, conv0, 2048)
    y_p = _ffn_down(act_p, wd, x1_p, g2, b2, 512)

    xs = x_sample.reshape(n_s * n_q, D_MODEL)
    p_s, k_s, v_s, _, _, gt_s = _inproj(xs, w_a, w_b, w_gate, n_s * n_q)
    hm_s, c_s, nn_s, mm_s = _mlstm(
        p_s, gt_s, bg, wn, state_C[0], state_n[0][:, :, None, :], state_m[0][:, :, None, None],
        batch=n_s, L=n_q, out_dtype=F32)
    hs_s = _sb_sample(p_s, k_s, v_s, cache_k.reshape(-1, DH_SB), cache_v.reshape(-1, DH_SB),
                      page_table, bsb, n_q=n_q)
    x1_s, x1b_s = _outproj(hm_s, hs_s, xs, wo, g1, b1, n_s * n_q)
    s0u, s0g = _halves(state_conv[0][:, 0, :])
    s1u, s1g = _halves(state_conv[0][:, 1, :])
    act_s, hu_s, hg_s = _ffn_up_sample(x1b_s, wu, wg, wcu, wcg, bcu, bcg, s0u, s1u, s0g, s1g, n_q)
    y_s = _ffn_down(act_s, wd, x1_s, g2, b2, n_s * n_q)

    conv_s = _unsplit(hu_s.reshape(n_s, n_q, D_FF_PAD)[:, n_q - 2:], hg_s.reshape(n_s, n_q, D_FF_PAD)[:, n_q - 2:])
    return (
        y_p.reshape(1, seq, D_MODEL),
        y_s.reshape(n_s, n_q, D_MODEL),
        k_p.reshape(1, 1, seq, H_SB, DH_SB),
        v_p.reshape(1, 1, seq, H_SB, DH_SB),
        c_p[None],
        nn_p.reshape(1, 1, H_M, DK_M),
        mm_p.reshape(1, 1, H_M),
        _unsplit(cu_p, cg_p)[None, None],
        k_s.reshape(1, n_s, n_q, H_SB, DH_SB),
        v_s.reshape(1, n_s, n_q, H_SB, DH_SB),
        c_s[None],
        nn_s.reshape(1, n_s, H_M, DK_M),
        mm_s.reshape(1, n_s, H_M),
        conv_s[None],
    )
```

```python
import functools

import jax
import jax.numpy as jnp
from jax import lax
from jax.experimental import pallas as pl
from jax.experimental.pallas import tpu as pltpu

F32 = jnp.float32
BF16 = jnp.bfloat16

D_MODEL = 2048
H_M = 4
DK_M = 128
DV_M = 256
W_M = H_M * DV_M
H_SB = 8
DH_SB = 128
W_SB = H_SB * DH_SB
D_FF = 5504
D_FF_PAD = 5632
CONV_W = 3
PAGE_SIZE = 128
LN_EPS = 1e-5
RMS_EPS = 1e-6
DN_ALPHA = 2.0 ** 0.25
GATE_COL = 3072
N_GATES = 2 * H_M
LANES = 128
MIB = 1024 * 1024
LOG2E = 1.4426950408889634


def _params(semantics, vmem_mib, flags=None):
    return pltpu.CompilerParams(dimension_semantics=semantics,
                                vmem_limit_bytes=int(vmem_mib * MIB), flags=flags)


def _dot(a, b):
    return jnp.dot(a, b, preferred_element_type=F32)


def _dot_nt(a, b):
    return lax.dot_general(a, b, (((1,), (1,)), ((), ())), preferred_element_type=F32)


def _dot_tn(a, b):
    return lax.dot_general(a, b, (((0,), (0,)), ((), ())), preferred_element_type=F32)


def _softplus(z):
    return jnp.maximum(z, 0.0) + jnp.log1p(jnp.exp(-jnp.abs(z)))


def _softplus2(z):
    return jnp.maximum(z, 0.0) + jnp.log2(1.0 + jnp.exp2(-jnp.abs(z)))


PREP_ROWS = 256


def _prep_up_kernel(u_ref, g_ref, wu_ref, wg_ref):
    pad = jnp.zeros((u_ref.shape[0], D_FF_PAD - D_FF), BF16)
    wu_ref[:, :D_FF] = u_ref[...].astype(BF16)
    wu_ref[:, D_FF:] = pad
    wg_ref[:, :D_FF] = g_ref[...].astype(BF16)
    wg_ref[:, D_FF:] = pad


def _prep_up(w_up):
    d = w_up.shape[0]
    return pl.pallas_call(
        _prep_up_kernel,
        grid=(d // PREP_ROWS,),
        in_specs=[pl.BlockSpec((PREP_ROWS, D_FF), lambda r: (r, 0)),
                  pl.BlockSpec((PREP_ROWS, D_FF), lambda r: (r, 1))],
        out_specs=[pl.BlockSpec((PREP_ROWS, D_FF_PAD), lambda r: (r, 0))] * 2,
        out_shape=[jax.ShapeDtypeStruct((d, D_FF_PAD), BF16)] * 2,
        compiler_params=_params(("parallel",), 40),
        name="prep_w_up",
    )(w_up, w_up)


IN_TN = 512
IN_NJ_A = GATE_COL // IN_TN
IN_NJ_P = 4096 // IN_TN
IN_NJ = IN_NJ_P + 4


def _inproj_kernel(x_ref, wa_ref, wb_ref, wg_ref, p_ref, k_ref, v_ref, kb_ref, vb_ref, g_ref, xb_ref):
    j = pl.program_id(1)

    @pl.when(j == 0)
    def _():
        xb = x_ref[...].astype(BF16)
        xb_ref[...] = xb
        g_ref[...] = _dot_nt(xb, wg_ref[...])

    @pl.when(j < IN_NJ_A)
    def _():
        p_ref[...] = _dot_nt(xb_ref[...], wa_ref[...])

    @pl.when(j >= IN_NJ_A)
    def _():
        acc = _dot_nt(xb_ref[...], wb_ref[...])

        @pl.when(j < IN_NJ_P)
        def _():
            p_ref[...] = acc

        @pl.when((j >= IN_NJ_P) & (j < IN_NJ_P + 2))
        def _():
            k_ref[...] = acc
            kb_ref[...] = acc.astype(BF16)

        @pl.when(j >= IN_NJ_P + 2)
        def _():
            v_ref[...] = acc
            vb_ref[...] = acc.astype(BF16)


def _inproj(x, w_a, w_b, w_gate, tm):
    m = x.shape[0]
    k_blk = lambda i, j: (i, jnp.clip(j - IN_NJ_P, 0, 1))
    v_blk = lambda i, j: (i, jnp.clip(j - IN_NJ_P - 2, 0, 1))
    return pl.pallas_call(
        _inproj_kernel,
        grid=(m // tm, IN_NJ),
        in_specs=[
            pl.BlockSpec((tm, D_MODEL), lambda i, j: (i, 0)),
            pl.BlockSpec((IN_TN, D_MODEL), lambda i, j: (jnp.minimum(j, IN_NJ_A - 1), 0)),
            pl.BlockSpec((IN_TN, D_MODEL), lambda i, j: (jnp.maximum(j - IN_NJ_A, 0), 0)),
            pl.BlockSpec((LANES, D_MODEL), lambda i, j: (0, 0)),
        ],
        out_specs=[
            pl.BlockSpec((tm, IN_TN), lambda i, j: (i, jnp.minimum(j, IN_NJ_P - 1))),
            pl.BlockSpec((tm, IN_TN), k_blk),
            pl.BlockSpec((tm, IN_TN), v_blk),
            pl.BlockSpec((tm, IN_TN), k_blk),
            pl.BlockSpec((tm, IN_TN), v_blk),
            pl.BlockSpec((tm, LANES), lambda i, j: (i, 0)),
        ],
        out_shape=[
            jax.ShapeDtypeStruct((m, 4096), F32),
            jax.ShapeDtypeStruct((m, W_SB), F32),
            jax.ShapeDtypeStruct((m, W_SB), F32),
            jax.ShapeDtypeStruct((m, W_SB), BF16),
            jax.ShapeDtypeStruct((m, W_SB), BF16),
            jax.ShapeDtypeStruct((m, LANES), F32),
        ],
        scratch_shapes=[pltpu.VMEM((tm, D_MODEL), BF16)],
        compiler_params=_params(("parallel", "arbitrary"), 54),
        name="inproj",
    )(x, w_a, w_b, w_gate)


def _mlstm_kernel(q_ref, k_ref, v_ref, o_ref, g_ref, bg_ref, wn_ref, c0_ref, n0_ref, m0_ref,
                  h_ref, cout_ref, nout_ref, mout_ref, c_s, n_s, m_s, *, L):
    c = pl.program_id(1)

    @pl.when(c == 0)
    def _():
        c_s[...] = c0_ref[0]
        n_s[...] = n0_ref[0]
        m_s[...] = m0_ref[0]

    g = g_ref[...] + bg_ref[...]
    row = lax.broadcasted_iota(jnp.int32, (L, L), 0)
    col = lax.broadcasted_iota(jnp.int32, (L, L), 1)
    causal = col <= row
    eye = col == row
    heads = range(H_M)
    dk = [slice(h * DK_M, (h + 1) * DK_M) for h in heads]
    dv = [slice(h * DV_M, (h + 1) * DV_M) for h in heads]
    qf = [q_ref[:, dk[h]] * (DK_M ** -0.5) for h in heads]
    kf = [k_ref[:, dk[h]] for h in heads]
    qb = [x.astype(BF16) for x in qf]
    kb = [x.astype(BF16) for x in kf]
    vb = [v_ref[:, dv[h]].astype(BF16) for h in heads]
    c_prev = [c_s[h] for h in heads]
    n_prev = [n_s[h] for h in heads]
    m_prev = [m_s[h] for h in heads]
    s_raw = [_dot_nt(qb[h], kb[h]) for h in heads]
    q_c = [_dot(qb[h], c_prev[h].astype(BF16)) for h in heads]

    w_intra, w_inter, w_end, decay, m_t, m_new = [], [], [], [], [], []
    for h in heads:
        li_col = g[:, h:h + 1]
        fg_col = g[:, H_M + h:H_M + h + 1]
        lf_col = -_softplus(-fg_col)
        li_row = jnp.sum(jnp.where(eye, li_col, 0.0), axis=0, keepdims=True)
        lf_row = jnp.sum(jnp.where(eye, lf_col, 0.0), axis=0, keepdims=True)
        b_row = jnp.sum(jnp.where(row <= col, lf_col, 0.0), axis=0, keepdims=True)
        b_col = jnp.sum(jnp.where(causal, lf_row, 0.0), axis=1, keepdims=True)
        b_last = b_col[L - 1:L, :]
        d = jnp.where(causal, b_col - b_row + li_row, -jnp.inf)
        a_col = b_col + m_prev[h]
        mt = jnp.maximum(a_col, jnp.max(d, axis=1, keepdims=True))
        mn = mt[L - 1:L, :]
        m_t.append(mt)
        m_new.append(mn)
        w_intra.append(jnp.exp(d - mt))
        w_inter.append(jnp.exp(a_col - mt))
        w_end.append(jnp.exp(b_last - b_col + li_col - mn))
        decay.append(jnp.exp(b_last + m_prev[h] - mn))

    kw = [kf[h] * w_end[h] for h in heads]
    k_v = [_dot_tn(kw[h].astype(BF16), vb[h]) for h in heads]
    s = [s_raw[h] * w_intra[h] for h in heads]
    s_v = [_dot(s[h].astype(BF16), vb[h]) for h in heads]
    for h in heads:
        c_s[h] = decay[h] * c_prev[h] + k_v[h]
        n_s[h] = decay[h] * n_prev[h] + jnp.sum(kw[h], axis=0, keepdims=True)
        m_s[h] = m_new[h]
    for h in heads:
        num = w_inter[h] * q_c[h] + s_v[h]
        den = (w_inter[h] * jnp.sum(qf[h] * n_prev[h], axis=1, keepdims=True)
               + jnp.sum(s[h], axis=1, keepdims=True))
        hh = num / jnp.maximum(jnp.abs(den), jnp.exp(-m_t[h]))
        hn = hh * lax.rsqrt(jnp.mean(hh * hh, axis=1, keepdims=True) + RMS_EPS)
        hn = hn * wn_ref[:, dv[h]]
        og = jax.nn.sigmoid(o_ref[:, dv[h]])
        h_ref[:, dv[h]] = (og * hn).astype(h_ref.dtype)

    @pl.when(c == pl.num_programs(1) - 1)
    def _():
        cout_ref[0] = c_s[...]
        nout_ref[0] = n_s[...]
        mout_ref[0] = m_s[...]


def _mlstm(p, gates, bg, wn, c0, n0, m0, *, batch, L, out_dtype):
    t_total = p.shape[0]
    nc = t_total // (batch * L)
    row = lambda b, c: b * nc + c
    return pl.pallas_call(
        functools.partial(_mlstm_kernel, L=L),
        grid=(batch, nc),
        in_specs=[
            pl.BlockSpec((L, H_M * DK_M), lambda b, c: (row(b, c), 0)),
            pl.BlockSpec((L, H_M * DK_M), lambda b, c: (row(b, c), 1)),
            pl.BlockSpec((L, W_M), lambda b, c: (row(b, c), 1)),
            pl.BlockSpec((L, W_M), lambda b, c: (row(b, c), 2)),
            pl.BlockSpec((L, LANES), lambda b, c: (row(b, c), 0)),
            pl.BlockSpec((1, LANES), lambda b, c: (0, 0)),
            pl.BlockSpec((1, W_M), lambda b, c: (0, 0)),
            pl.BlockSpec((1, H_M, DK_M, DV_M), lambda b, c: (b, 0, 0, 0)),
            pl.BlockSpec((1, H_M, 1, DK_M), lambda b, c: (b, 0, 0, 0)),
            pl.BlockSpec((1, H_M, 1, 1), lambda b, c: (b, 0, 0, 0)),
        ],
        out_specs=[
            pl.BlockSpec((L, W_M), lambda b, c: (row(b, c), 0)),
            pl.BlockSpec((1, H_M, DK_M, DV_M), lambda b, c: (b, 0, 0, 0)),
            pl.BlockSpec((1, H_M, 1, DK_M), lambda b, c: (b, 0, 0, 0)),
            pl.BlockSpec((1, H_M, 1, 1), lambda b, c: (b, 0, 0, 0)),
        ],
        out_shape=[
            jax.ShapeDtypeStruct((t_total, W_M), out_dtype),
            jax.ShapeDtypeStruct((batch, H_M, DK_M, DV_M), F32),
            jax.ShapeDtypeStruct((batch, H_M, 1, DK_M), F32),
            jax.ShapeDtypeStruct((batch, H_M, 1, 1), F32),
        ],
        scratch_shapes=[
            pltpu.VMEM((H_M, DK_M, DV_M), F32),
            pltpu.VMEM((H_M, 1, DK_M), F32),
            pltpu.VMEM((H_M, 1, 1), F32),
        ],
        compiler_params=_params(("parallel", "arbitrary"), 32),
        name="mlstm",
    )(p, p, p, p, gates, bg, wn, c0, n0, m0)


SBP_BQ = 512
SBP_BK = 256
SBP_NH = 4
SB_QSCALE = (DH_SB ** -0.5) * LOG2E


def _sb_prompt_kernel(bsb_ref, q_ref, k_ref, v_ref, o_ref, acc_ref, rest_ref,
                      u0_ref, u1_ref, t0_ref, t1_ref):
    hp = pl.program_id(0)
    i = pl.program_id(1)
    bq, bk = SBP_BQ, SBP_BK
    per_q = bq // bk
    nk = per_q * (i + 1)
    heads = [slice(hh * DH_SB, (hh + 1) * DH_SB) for hh in range(SBP_NH)]
    qs = [(q_ref[:, hs] * SB_QSCALE).astype(BF16) for hs in heads]
    bias = [bsb_ref[hp * SBP_NH + hh] * LOG2E for hh in range(SBP_NH)]
    later_aug = _later_aug(bk)
    row = lax.broadcasted_iota(jnp.int32, (bq, bk), 0)
    col = lax.broadcasted_iota(jnp.int32, (bq, bk), 1)
    acc_ref[...] = jnp.zeros_like(acc_ref)
    rest_ref[...] = jnp.zeros_like(rest_ref)

    def key_rows(jj):
        return pl.ds(pl.multiple_of((nk - 1 - jj) * bk, bk), bk)

    def scores(jj, u_ref, t_ref, diag):
        rows = key_rows(jj)
        mask = None if diag is None else (col + bk * (per_q - 1 - diag)) < row
        zs = [_dot_nt(qs[hh], k_ref[rows, hs]) + bias[hh] for hh, hs in enumerate(heads)]
        sps = [_softplus2(z) for z in zs]
        spms = sps if mask is None else [jnp.where(mask, sp, 0.0) for sp in sps]
        eas = [_dot(spm.astype(BF16), later_aug) for spm in spms]
        for hh in range(SBP_NH):
            u = zs[hh] - sps[hh] - eas[hh][:, :bk]
            u_ref[hh] = u if mask is None else jnp.where(mask, u, -jnp.inf)
            t_ref[hh] = eas[hh][:, bk:]

    def values(jj, u_ref, t_ref):
        rows = key_rows(jj)
        rests = [rest_ref[hh] for hh in range(SBP_NH)]
        a_s = [jnp.exp2(u_ref[hh] + jnp.concatenate([rests[hh]] * (bk // LANES), axis=1))
               for hh in range(SBP_NH)]
        outs = [_dot(a_s[hh].astype(BF16), v_ref[rows, hs]) for hh, hs in enumerate(heads)]
        for hh in range(SBP_NH):
            acc_ref[hh] += outs[hh]
            rest_ref[hh] = rests[hh] - t_ref[hh]

    scores(0, u0_ref, t0_ref, 0)
    scores(1, u1_ref, t1_ref, 1)
    values(0, u0_ref, t0_ref)

    def fused(jj, ua_ref, ta_ref, ub_ref, tb_ref):
        rows_a, rows_b = key_rows(jj), key_rows(jj - 1)
        nh = range(SBP_NH)
        zs = [_dot_nt(qs[hh], k_ref[rows_a, hs]) + bias[hh] for hh, hs in enumerate(heads)]
        rests = [rest_ref[hh] for hh in nh]
        a_s = [jnp.exp2(ub_ref[hh] + jnp.concatenate([rests[hh]] * (bk // LANES), axis=1)) for hh in nh]
        outs = [_dot(a_s[hh].astype(BF16), v_ref[rows_b, hs]) for hh, hs in enumerate(heads)]
        sps = [_softplus2(z) for z in zs]
        eas = [_dot(sp.astype(BF16), later_aug) for sp in sps]
        for hh in nh:
            acc_ref[hh] += outs[hh]
            rest_ref[hh] = rests[hh] - tb_ref[hh]
        for hh in nh:
            ua_ref[hh] = zs[hh] - sps[hh] - eas[hh][:, :bk]
            ta_ref[hh] = eas[hh][:, bk:]

    def pair(p, carry):
        jj = per_q * (p + 1)
        fused(jj, u0_ref, t0_ref, u1_ref, t1_ref)
        fused(jj + 1, u1_ref, t1_ref, u0_ref, t0_ref)
        return carry

    lax.fori_loop(0, i, pair, 0)
    values(nk - 1, u1_ref, t1_ref)
    for hh, hs in enumerate(heads):
        o_ref[:, hs] = acc_ref[hh].astype(o_ref.dtype)


def _sb_prompt(p, kb, vb, b_sb):
    assert SBP_BQ == 2 * SBP_BK
    T = p.shape[0]
    width = SBP_NH * DH_SB
    q_col0 = 3072 // width
    u_shape = pltpu.VMEM((SBP_NH, SBP_BQ, SBP_BK), F32)
    t_shape = pltpu.VMEM((SBP_NH, SBP_BQ, LANES), F32)
    return pl.pallas_call(
        _sb_prompt_kernel,
        grid=(H_SB // SBP_NH, T // SBP_BQ),
        in_specs=[
            pl.BlockSpec(memory_space=pltpu.SMEM),
            pl.BlockSpec((SBP_BQ, width), lambda h, i: (i, q_col0 + h)),
            pl.BlockSpec((T, width), lambda h, i: (0, h)),
            pl.BlockSpec((T, width), lambda h, i: (0, h)),
        ],
        out_specs=pl.BlockSpec((SBP_BQ, width), lambda h, i: (i, h)),
        out_shape=jax.ShapeDtypeStruct((T, W_SB), BF16),
        scratch_shapes=[pltpu.VMEM((SBP_NH, SBP_BQ, DH_SB), F32), t_shape,
                        u_shape, u_shape, t_shape, t_shape],
        compiler_params=_params(("parallel", "arbitrary"), 52),
        name="sb_prompt",
    )(b_sb, p, kb, vb)


SBS_PPS = 16


def _later_aug(n):
    row = lax.broadcasted_iota(jnp.int32, (n, n + LANES), 0)
    col = lax.broadcasted_iota(jnp.int32, (n, n + LANES), 1)
    return ((row > col) | (col >= n)).astype(BF16)


def _sb_sample_kernel(pt_ref, bsb_ref, q_ref, kn_ref, vn_ref, *rest, n_q):
    page_refs = rest[:2 * SBS_PPS]
    o_ref, acc_ref, rest_ref = rest[2 * SBS_PPS:]
    g = pl.program_id(1)
    rows = H_SB * n_q

    qs = [(q_ref[:, h * DH_SB:(h + 1) * DH_SB] * SB_QSCALE).astype(BF16) for h in range(H_SB)]
    bias = jnp.concatenate([jnp.full((n_q, PAGE_SIZE), bsb_ref[h] * LOG2E, F32) for h in range(H_SB)],
                           axis=0)
    later_aug = _later_aug(PAGE_SIZE)

    def head_dots_nt(ks):
        return jnp.concatenate([_dot_nt(qs[h], ks[h]) for h in range(H_SB)], axis=0)

    def head_dots(ab, vs):
        return jnp.concatenate([_dot(ab[h * n_q:(h + 1) * n_q, :], vs[h]) for h in range(H_SB)], axis=0)

    @pl.when(g == 0)
    def _():
        pad = jnp.zeros((PAGE_SIZE - n_q, DH_SB), BF16)
        ks = [jnp.concatenate([kn_ref[:, h * DH_SB:(h + 1) * DH_SB].astype(BF16), pad], axis=0)
              for h in range(H_SB)]
        vs = [jnp.concatenate([vn_ref[:, h * DH_SB:(h + 1) * DH_SB].astype(BF16), pad], axis=0)
              for h in range(H_SB)]
        r = lax.broadcasted_iota(jnp.int32, (rows, PAGE_SIZE), 0)
        cidx = lax.broadcasted_iota(jnp.int32, (rows, PAGE_SIZE), 1)
        mask = cidx < (r % n_q)
        z = head_dots_nt(ks) + bias
        sp = _softplus2(z)
        ea = _dot(jnp.where(mask, sp, 0.0).astype(BF16), later_aug)
        a = jnp.where(mask, jnp.exp2(z - sp - ea[:, :PAGE_SIZE]), 0.0)
        acc_ref[...] = head_dots(a.astype(BF16), vs)
        rest_ref[...] = -ea[:, PAGE_SIZE:]

    def head_rows(refs, h):
        return jnp.concatenate([r[pl.ds(h, PAGE_SIZE, stride=H_SB), :] for r in refs],
                               axis=0).astype(BF16)

    ks = [head_rows(page_refs[:SBS_PPS], h) for h in range(H_SB)]
    vs = [head_rows(page_refs[SBS_PPS:], h) for h in range(H_SB)]
    z = head_dots_nt(ks) + jnp.concatenate([bias] * SBS_PPS, axis=1)
    sp = _softplus2(z)
    spb = sp.astype(BF16)
    rest_v = rest_ref[...]
    es, rests = [], []
    for c in range(SBS_PPS):
        ea = _dot(spb[:, c * PAGE_SIZE:(c + 1) * PAGE_SIZE], later_aug)
        es.append(ea[:, :PAGE_SIZE])
        rests.append(rest_v)
        rest_v = rest_v - ea[:, PAGE_SIZE:]
    a = jnp.exp2(z - sp - jnp.concatenate(es, axis=1) + jnp.concatenate(rests, axis=1))
    acc = acc_ref[...] + head_dots(a.astype(BF16), vs)
    acc_ref[...] = acc
    rest_ref[...] = rest_v

    @pl.when(g == pl.num_programs(1) - 1)
    def _():
        for h in range(H_SB):
            o_ref[:, h * DH_SB:(h + 1) * DH_SB] = acc[h * n_q:(h + 1) * n_q, :]


def _sb_sample(p, k_new, v_new, cache_k, cache_v, page_table, b_sb, *, n_q):
    n_seq, n_pages = page_table.shape
    n_groups = n_pages // SBS_PPS

    def page_spec(c):
        return pl.BlockSpec(
            (PAGE_SIZE * H_SB, DH_SB),
            lambda b, g, pt: (pt[b, n_pages - 1 - (g * SBS_PPS + c)], 0))

    return pl.pallas_call(
        functools.partial(_sb_sample_kernel, n_q=n_q),
        grid_spec=pltpu.PrefetchScalarGridSpec(
            num_scalar_prefetch=1,
            grid=(n_seq, n_groups),
            in_specs=[
                pl.BlockSpec(memory_space=pltpu.SMEM),
                pl.BlockSpec((n_q, W_SB), lambda b, g, pt: (b, 3)),
                pl.BlockSpec((n_q, W_SB), lambda b, g, pt: (b, 0)),
                pl.BlockSpec((n_q, W_SB), lambda b, g, pt: (b, 0)),
            ] + [page_spec(c) for c in range(SBS_PPS)] * 2,
            out_specs=pl.BlockSpec((n_q, W_SB), lambda b, g, pt: (b, 0)),
            scratch_shapes=[pltpu.VMEM((H_SB * n_q, DH_SB), F32),
                            pltpu.VMEM((H_SB * n_q, LANES), F32)],
        ),
        out_shape=jax.ShapeDtypeStruct((n_seq * n_q, W_SB), F32),
        compiler_params=_params(("parallel", "arbitrary"), 48),
        name="sb_sample",
    )(page_table, b_sb, p, k_new, v_new,
      *([cache_k] * SBS_PPS), *([cache_v] * SBS_PPS))


ROW_CHUNK = 256


def _layer_norm(x, g, b):
    mu = jnp.mean(x, axis=-1, keepdims=True)
    xc = x - mu
    var = jnp.mean(xc * xc, axis=-1, keepdims=True)
    return xc * lax.rsqrt(var + LN_EPS) * g + b


def _outproj_kernel(hm_ref, hs_ref, x_ref, wm_ref, ws_ref, g_ref, b_ref, x1_ref, x1b_ref, *, tm):
    for c in range(tm // ROW_CHUNK):
        rows = slice(c * ROW_CHUNK, (c + 1) * ROW_CHUNK)
        mix = (_dot(hm_ref[rows, :].astype(BF16), wm_ref[...])
               + _dot(hs_ref[rows, :].astype(BF16), ws_ref[...]))
        x1 = _layer_norm(DN_ALPHA * x_ref[rows, :] + mix, g_ref[...], b_ref[...])
        x1_ref[rows, :] = x1
        x1b_ref[rows, :] = x1.astype(BF16)


def _outproj(hm, hs, x, w_out, ln_g, ln_b, tm):
    m = x.shape[0]
    return pl.pallas_call(
        functools.partial(_outproj_kernel, tm=tm),
        grid=(m // tm,),
        in_specs=[
            pl.BlockSpec((tm, W_M), lambda i: (i, 0)),
            pl.BlockSpec((tm, W_SB), lambda i: (i, 0)),
            pl.BlockSpec((tm, D_MODEL), lambda i: (i, 0)),
            pl.BlockSpec((W_M, D_MODEL), lambda i: (0, 0)),
            pl.BlockSpec((W_SB, D_MODEL), lambda i: (1, 0)),
            pl.BlockSpec((1, D_MODEL), lambda i: (0, 0)),
            pl.BlockSpec((1, D_MODEL), lambda i: (0, 0)),
        ],
        out_specs=[
            pl.BlockSpec((tm, D_MODEL), lambda i: (i, 0)),
            pl.BlockSpec((tm, D_MODEL), lambda i: (i, 0)),
        ],
        out_shape=[
            jax.ShapeDtypeStruct((m, D_MODEL), F32),
            jax.ShapeDtypeStruct((m, D_MODEL), BF16),
        ],
        compiler_params=_params(("parallel",), 48),
        name="outproj_ln",
    )(hm, hs, x, w_out, w_out, ln_g, ln_b)


UP_TN = 512
UP_NJ = D_FF_PAD // UP_TN
UP_CHUNK = 256
CARRY = 8


GELU_C1 = 0.7978845608028654
GELU_C3 = GELU_C1 * 0.044715


def _gelu_tanh(x):
    half = 0.5 * x
    return half + half * jnp.tanh(x * (GELU_C1 + GELU_C3 * (x * x)))


def _conv_taps(buf_ref, h, off, wc_ref, bc_ref):
    n = h.shape[0]
    buf_ref[CARRY + off:CARRY + off + n, :] = h
    h1 = buf_ref[CARRY + off - 1:CARRY + off - 1 + n, :]
    h2 = buf_ref[CARRY + off - 2:CARRY + off - 2 + n, :]
    return bc_ref[...] + wc_ref[0:1, :] * h2 + wc_ref[1:2, :] * h1 + wc_ref[2:3, :] * h


def _ffn_up_prompt_kernel(x_ref, wu_ref, wg_ref, wcu_ref, wcg_ref, bcu_ref, bcg_ref, pu_ref, pg_ref,
                          act_ref, cu_ref, cg_ref, bu_s, bg_s, *, tm):
    i = pl.program_id(1)

    @pl.when(i == 0)
    def _():
        bu_s[CARRY - 2:CARRY, :] = pu_ref[...]
        bg_s[CARRY - 2:CARRY, :] = pg_ref[...]

    @pl.when(i > 0)
    def _():
        bu_s[0:CARRY, :] = bu_s[tm:tm + CARRY, :]
        bg_s[0:CARRY, :] = bg_s[tm:tm + CARRY, :]

    for c in range(tm // UP_CHUNK):
        off = c * UP_CHUNK
        x = x_ref[off:off + UP_CHUNK, :]
        hg = _dot(x, wg_ref[...])
        hu = _dot(x, wu_ref[...])
        gate = _gelu_tanh(_conv_taps(bg_s, hg, off, wcg_ref, bcg_ref))
        cu = _conv_taps(bu_s, hu, off, wcu_ref, bcu_ref)
        act_ref[off:off + UP_CHUNK, :] = (gate * cu).astype(act_ref.dtype)
    cu_ref[...] = hu[UP_CHUNK - 2:UP_CHUNK, :]
    cg_ref[...] = hg[UP_CHUNK - 2:UP_CHUNK, :]


def _ffn_up_prompt(x1b, wu, wg, wcu, wcg, bcu, bcg, pu, pg, tm):
    m = x1b.shape[0]
    col = lambda j, i: (0, j)
    return pl.pallas_call(
        functools.partial(_ffn_up_prompt_kernel, tm=tm),
        grid=(UP_NJ, m // tm),
        in_specs=[
            pl.BlockSpec((tm, D_MODEL), lambda j, i: (i, 0)),
            pl.BlockSpec((D_MODEL, UP_TN), col),
            pl.BlockSpec((D_MODEL, UP_TN), col),
            pl.BlockSpec((CONV_W, UP_TN), col),
            pl.BlockSpec((CONV_W, UP_TN), col),
            pl.BlockSpec((1, UP_TN), col),
            pl.BlockSpec((1, UP_TN), col),
            pl.BlockSpec((CONV_W - 1, UP_TN), col),
            pl.BlockSpec((CONV_W - 1, UP_TN), col),
        ],
        out_specs=[
            pl.BlockSpec((tm, UP_TN), lambda j, i: (i, j)),
            pl.BlockSpec((CONV_W - 1, UP_TN), col),
            pl.BlockSpec((CONV_W - 1, UP_TN), col),
        ],
        out_shape=[
            jax.ShapeDtypeStruct((m, D_FF_PAD), BF16),
            jax.ShapeDtypeStruct((CONV_W - 1, D_FF_PAD), F32),
            jax.ShapeDtypeStruct((CONV_W - 1, D_FF_PAD), F32),
        ],
        scratch_shapes=[pltpu.VMEM((tm + CARRY, UP_TN), F32), pltpu.VMEM((tm + CARRY, UP_TN), F32)],
        compiler_params=_params(("parallel", "arbitrary"), 48),
        name="ffn_up_prompt",
    )(x1b, wu, wg, wcu, wcg, bcu, bcg, pu, pg)


def _conv_taps_seq(buf_ref, h, s0_ref, s1_ref, wc_ref, bc_ref, seg):
    n = h.shape[0]
    n_seq = n // seg
    h1s, h2s = [], []
    for s in range(h.shape[1] // LANES):
        lanes = slice(s * LANES, (s + 1) * LANES)
        buf_ref[s, CARRY:CARRY + n, :] = h[:, lanes]
        buf_ref[s, pl.ds(CARRY - 1, n_seq, stride=seg), :] = s1_ref[:, lanes]
        h1s.append(buf_ref[s, CARRY - 1:CARRY - 1 + n, :])
        buf_ref[s, pl.ds(CARRY - 2, n_seq, stride=seg), :] = s0_ref[:, lanes]
        h2s.append(buf_ref[s, CARRY - 2:CARRY - 2 + n, :])
    h1 = jnp.concatenate(h1s, axis=1)
    h2 = jnp.concatenate(h2s, axis=1)
    return bc_ref[...] + wc_ref[0:1, :] * h2 + wc_ref[1:2, :] * h1 + wc_ref[2:3, :] * h


def _ffn_up_sample_kernel(x_ref, wu_ref, wg_ref, wcu_ref, wcg_ref, bcu_ref, bcg_ref,
                          s0u_ref, s1u_ref, s0g_ref, s1g_ref,
                          act_ref, hu_ref, hg_ref, bu_s, bg_s, *, seg):
    x = x_ref[...]
    hu = _dot(x, wu_ref[...])
    hg = _dot(x, wg_ref[...])
    cu = _conv_taps_seq(bu_s, hu, s0u_ref, s1u_ref, wcu_ref, bcu_ref, seg)
    cg = _conv_taps_seq(bg_s, hg, s0g_ref, s1g_ref, wcg_ref, bcg_ref, seg)
    act_ref[...] = (_gelu_tanh(cg) * cu).astype(act_ref.dtype)
    hu_ref[...] = hu
    hg_ref[...] = hg


def _ffn_up_sample(x1b, wu, wg, wcu, wcg, bcu, bcg, s0u, s1u, s0g, s1g, seg):
    m = x1b.shape[0]
    n_seq = m // seg
    col = lambda j: (0, j)
    w_spec = pl.BlockSpec((D_MODEL, UP_TN), col)
    c_spec = pl.BlockSpec((CONV_W, UP_TN), col)
    b_spec = pl.BlockSpec((1, UP_TN), col)
    s_spec = pl.BlockSpec((n_seq, UP_TN), col)
    o_spec = pl.BlockSpec((m, UP_TN), col)
    return pl.pallas_call(
        functools.partial(_ffn_up_sample_kernel, seg=seg),
        grid=(UP_NJ,),
        in_specs=[pl.BlockSpec((m, D_MODEL), lambda j: (0, 0)), w_spec, w_spec, c_spec, c_spec,
                  b_spec, b_spec, s_spec, s_spec, s_spec, s_spec],
        out_specs=[o_spec, o_spec, o_spec],
        out_shape=[
            jax.ShapeDtypeStruct((m, D_FF_PAD), BF16),
            jax.ShapeDtypeStruct((m, D_FF_PAD), F32),
            jax.ShapeDtypeStruct((m, D_FF_PAD), F32),
        ],
        scratch_shapes=[pltpu.VMEM((UP_TN // LANES, m + CARRY, LANES), F32)] * 2,
        compiler_params=_params(("parallel",), 32),
        name="ffn_up_sample",
    )(x1b, wu, wg, wcu, wcg, bcu, bcg, s0u, s1u, s0g, s1g)


DOWN_TK = 1408
DOWN_NK = D_FF_PAD // DOWN_TK
DOWN_LAST = D_FF - (DOWN_NK - 1) * DOWN_TK


def _ffn_down_kernel(a_ref, w_ref, x1_ref, g_ref, b_ref, y_ref, acc_ref):
    k = pl.program_id(1)

    @pl.when(k == 0)
    def _():
        acc_ref[...] = _dot(a_ref[...], w_ref[...])

    @pl.when((k > 0) & (k < DOWN_NK - 1))
    def _():
        acc_ref[...] += _dot(a_ref[...], w_ref[...])

    @pl.when(k == DOWN_NK - 1)
    def _():
        ffn = acc_ref[...] + _dot(a_ref[:, :DOWN_LAST], w_ref[:DOWN_LAST, :])
        y_ref[...] = _layer_norm(DN_ALPHA * x1_ref[...] + ffn, g_ref[...], b_ref[...])


def _ffn_down(act, w_down, x1, ln_g, ln_b, tm):
    m = act.shape[0]
    return pl.pallas_call(
        _ffn_down_kernel,
        grid=(m // tm, DOWN_NK),
        in_specs=[
            pl.BlockSpec((tm, DOWN_TK), lambda i, k: (i, k)),
            pl.BlockSpec((DOWN_TK, D_MODEL), lambda i, k: (k, 0)),
            pl.BlockSpec((tm, D_MODEL), lambda i, k: (i, 0)),
            pl.BlockSpec((1, D_MODEL), lambda i, k: (0, 0)),
            pl.BlockSpec((1, D_MODEL), lambda i, k: (0, 0)),
        ],
        out_specs=pl.BlockSpec((tm, D_MODEL), lambda i, k: (i, 0)),
        out_shape=jax.ShapeDtypeStruct((m, D_MODEL), F32),
        scratch_shapes=[pltpu.VMEM((tm, D_MODEL), F32)],
        compiler_params=_params(("parallel", "arbitrary"), 48),
        name="ffn_down_ln",
    )(act, w_down, x1, ln_g, ln_b)


def _pad_cols(a, width):
    return jnp.pad(a, ((0, 0), (0, width - a.shape[1])))


def _halves(a, dtype=F32):
    return (_pad_cols(a[:, :D_FF], D_FF_PAD).astype(dtype), _pad_cols(a[:, D_FF:], D_FF_PAD).astype(dtype))


def _unsplit(u, g):
    return jnp.concatenate([u[..., :D_FF], g[..., :D_FF]], axis=-1)


def kernel(x_prompt, x_sample, cache_k, cache_v, state_C, state_n, state_m, state_conv, page_table,
           w_in, b_gates, b_sb, w_mlstm_norm, w_out, ln1_g, ln1_b, w_up, w_conv, b_conv, w_down,
           ln2_g, ln2_b):
    n_b, seq, _ = x_prompt.shape
    n_s, n_q, _ = x_sample.shape
    assert n_b == 1

    wi = w_in[0]
    wi_t = wi.T
    w_a = wi_t[:GATE_COL].astype(BF16)
    w_b = wi_t[GATE_COL + N_GATES:].astype(BF16)
    w_gate = jnp.pad(wi_t[GATE_COL:GATE_COL + N_GATES], ((0, LANES - N_GATES), (0, 0))).astype(BF16)
    bg = _pad_cols(b_gates[0][None, :], LANES)
    wn = w_mlstm_norm[0][None, :]
    wo = w_out[0].astype(BF16)
    wu, wg = _prep_up(w_up[0])
    wcu, wcg = _halves(w_conv[0])
    bcu, bcg = _halves(b_conv[0][None, :])
    wd = w_down[0].astype(BF16)
    g1, b1 = ln1_g[0][None, :], ln1_b[0][None, :]
    g2, b2 = ln2_g[0][None, :], ln2_b[0][None, :]
    bsb = b_sb[0]

    xp = x_prompt.reshape(seq, D_MODEL)
    p_p, k_p, v_p, kb_p, vb_p, gt_p = _inproj(xp, w_a, w_b, w_gate, 1024)
    hm_p, c_p, nn_p, mm_p = _mlstm(
        p_p, gt_p, bg, wn,
        jnp.zeros((1, H_M, DK_M, DV_M), F32), jnp.zeros((1, H_M, 1, DK_M), F32),
        jnp.zeros((1, H_M, 1, 1), F32), batch=1, L=256, out_dtype=BF16)
    hs_p = _sb_prompt(p_p, kb_p, vb_p, bsb)
    x1_p, x1b_p = _outproj(hm_p, hs_p, xp, wo, g1, b1, 512)
    conv0 = jnp.zeros((CONV_W - 1, D_FF_PAD), F32)
    act_p, cu_p, cg_p = _ffn_up_prompt(x1b_p, wu, wg, wcu, wcg, bcu, bcg, conv0, conv0, 2048)
    y_p = _ffn_down(act_p, wd, x1_p, g2, b2, 512)

    xs = x_sample.reshape(n_s * n_q, D_MODEL)
    p_s, k_s, v_s, _, _, gt_s = _inproj(xs, w_a, w_b, w_gate, n_s * n_q)
    hm_s, c_s, nn_s, mm_s = _mlstm(
        p_s, gt_s, bg, wn, state_C[0], state_n[0][:, :, None, :], state_m[0][:, :, None, None],
        batch=n_s, L=n_q, out_dtype=F32)
    hs_s = _sb_sample(p_s, k_s, v_s, cache_k.reshape(-1, DH_SB), cache_v.reshape(-1, DH_SB),
                      page_table, bsb, n_q=n_q)
    x1_s, x1b_s = _outproj(hm_s, hs_s, xs, wo, g1, b1, n_s * n_q)
    s0u, s0g = _halves(state_conv[0][:, 0, :])
    s1u, s1g = _halves(state_conv[0][:, 1, :])
    act_s, hu_s, hg_s = _ffn_up_sample(x1b_s, wu, wg, wcu, wcg, bcu, bcg, s0u, s1u, s0g, s1g, n_q)
    y_s = _ffn_down(act_s, wd, x1_s, g2, b2, n_s * n_q)

    conv_s = _unsplit(hu_s.reshape(n_s, n_q, D_FF_PAD)[:, n_q - 2:], hg_s.reshape(n_s, n_q, D_FF_PAD)[:, n_q - 2:])
    return (
        y_p.reshape(1, seq, D_MODEL),
        y_s.reshape(n_s, n_q, D_MODEL),
        k_p.reshape(1, 1, seq, H_SB, DH_SB),
        v_p.reshape(1, 1, seq, H_SB, DH_SB),
        c_p[None],
        nn_p.reshape(1, 1, H_M, DK_M),
        mm_p.reshape(1, 1, H_M),
        _unsplit(cu_p, cg_p)[None, None],
        k_s.reshape(1, n_s, n_q, H_SB, DH_SB),
        v_s.reshape(1, n_s, n_q, H_SB, DH_SB),
        c_s[None],
        nn_s.reshape(1, n_s, H_M, DK_M),
        mm_s.reshape(1, n_s, H_M),
        conv_s[None],
    )
```

```python
import functools

import jax
import jax.numpy as jnp
from jax import lax
from jax.experimental import pallas as pl
from jax.experimental.pallas import tpu as pltpu

F32 = jnp.float32
BF16 = jnp.bfloat16

D_MODEL = 2048
H_M = 4
DK_M = 128
DV_M = 256
W_M = H_M * DV_M
H_SB = 8
DH_SB = 128
W_SB = H_SB * DH_SB
D_FF = 5504
D_FF_PAD = 5632
CONV_W = 3
PAGE_SIZE = 128
LN_EPS = 1e-5
RMS_EPS = 1e-6
DN_ALPHA = 2.0 ** 0.25
GATE_COL = 3072
N_GATES = 2 * H_M
LANES = 128
MIB = 1024 * 1024
LOG2E = 1.4426950408889634


def _params(semantics, vmem_mib, flags=None):
    return pltpu.CompilerParams(dimension_semantics=semantics,
                                vmem_limit_bytes=int(vmem_mib * MIB), flags=flags)


def _dot(a, b):
    return jnp.dot(a, b, preferred_element_type=F32)


def _dot_nt(a, b):
    return lax.dot_general(a, b, (((1,), (1,)), ((), ())), preferred_element_type=F32)


def _dot_tn(a, b):
    return lax.dot_general(a, b, (((0,), (0,)), ((), ())), preferred_element_type=F32)


def _softplus(z):
    return jnp.maximum(z, 0.0) + jnp.log1p(jnp.exp(-jnp.abs(z)))


def _softplus2(z):
    return jnp.maximum(z, 0.0) + jnp.log2(1.0 + jnp.exp2(-jnp.abs(z)))


IN_TN = 512
IN_NJ_A = GATE_COL // IN_TN
IN_NJ_P = 4096 // IN_TN
IN_NJ = IN_NJ_P + 4


def _inproj_kernel(x_ref, wa_ref, wb_ref, wg_ref, p_ref, k_ref, v_ref, kb_ref, vb_ref, g_ref, xb_ref):
    j = pl.program_id(1)

    @pl.when(j == 0)
    def _():
        xb = x_ref[...].astype(BF16)
        xb_ref[...] = xb
        g_ref[...] = _dot_nt(xb, wg_ref[...])

    @pl.when(j < IN_NJ_A)
    def _():
        p_ref[...] = _dot_nt(xb_ref[...], wa_ref[...])

    @pl.when(j >= IN_NJ_A)
    def _():
        acc = _dot_nt(xb_ref[...], wb_ref[...])

        @pl.when(j < IN_NJ_P)
        def _():
            p_ref[...] = acc

        @pl.when((j >= IN_NJ_P) & (j < IN_NJ_P + 2))
        def _():
            k_ref[...] = acc
            kb_ref[...] = acc.astype(BF16)

        @pl.when(j >= IN_NJ_P + 2)
        def _():
            v_ref[...] = acc
            vb_ref[...] = acc.astype(BF16)


def _inproj(x, w_a, w_b, w_gate, tm):
    m = x.shape[0]
    k_blk = lambda i, j: (i, jnp.clip(j - IN_NJ_P, 0, 1))
    v_blk = lambda i, j: (i, jnp.clip(j - IN_NJ_P - 2, 0, 1))
    return pl.pallas_call(
        _inproj_kernel,
        grid=(m // tm, IN_NJ),
        in_specs=[
            pl.BlockSpec((tm, D_MODEL), lambda i, j: (i, 0)),
            pl.BlockSpec((IN_TN, D_MODEL), lambda i, j: (jnp.minimum(j, IN_NJ_A - 1), 0)),
            pl.BlockSpec((IN_TN, D_MODEL), lambda i, j: (jnp.maximum(j - IN_NJ_A, 0), 0)),
            pl.BlockSpec((LANES, D_MODEL), lambda i, j: (0, 0)),
        ],
        out_specs=[
            pl.BlockSpec((tm, IN_TN), lambda i, j: (i, jnp.minimum(j, IN_NJ_P - 1))),
            pl.BlockSpec((tm, IN_TN), k_blk),
            pl.BlockSpec((tm, IN_TN), v_blk),
            pl.BlockSpec((tm, IN_TN), k_blk),
            pl.BlockSpec((tm, IN_TN), v_blk),
            pl.BlockSpec((tm, LANES), lambda i, j: (i, 0)),
        ],
        out_shape=[
            jax.ShapeDtypeStruct((m, 4096), F32),
            jax.ShapeDtypeStruct((m, W_SB), F32),
            jax.ShapeDtypeStruct((m, W_SB), F32),
            jax.ShapeDtypeStruct((m, W_SB), BF16),
            jax.ShapeDtypeStruct((m, W_SB), BF16),
            jax.ShapeDtypeStruct((m, LANES), F32),
        ],
        scratch_shapes=[pltpu.VMEM((tm, D_MODEL), BF16)],
        compiler_params=_params(("parallel", "arbitrary"), 54),
        name="inproj",
    )(x, w_a, w_b, w_gate)


def _mlstm_kernel(q_ref, k_ref, v_ref, o_ref, g_ref, bg_ref, wn_ref, c0_ref, n0_ref, m0_ref,
                  h_ref, cout_ref, nout_ref, mout_ref, c_s, n_s, m_s, *, L):
    c = pl.program_id(1)

    @pl.when(c == 0)
    def _():
        c_s[...] = c0_ref[0]
        n_s[...] = n0_ref[0]
        m_s[...] = m0_ref[0]

    g = g_ref[...] + bg_ref[...]
    row = lax.broadcasted_iota(jnp.int32, (L, L), 0)
    col = lax.broadcasted_iota(jnp.int32, (L, L), 1)
    causal = col <= row
    eye = col == row
    heads = range(H_M)
    dk = [slice(h * DK_M, (h + 1) * DK_M) for h in heads]
    dv = [slice(h * DV_M, (h + 1) * DV_M) for h in heads]
    qf = [q_ref[:, dk[h]] * (DK_M ** -0.5) for h in heads]
    kf = [k_ref[:, dk[h]] for h in heads]
    qb = [x.astype(BF16) for x in qf]
    kb = [x.astype(BF16) for x in kf]
    vb = [v_ref[:, dv[h]].astype(BF16) for h in heads]
    c_prev = [c_s[h] for h in heads]
    n_prev = [n_s[h] for h in heads]
    m_prev = [m_s[h] for h in heads]
    s_raw = [_dot_nt(qb[h], kb[h]) for h in heads]
    q_c = [_dot(qb[h], c_prev[h].astype(BF16)) for h in heads]

    w_intra, w_inter, w_end, decay, m_t, m_new = [], [], [], [], [], []
    for h in heads:
        li_col = g[:, h:h + 1]
        fg_col = g[:, H_M + h:H_M + h + 1]
        lf_col = -_softplus(-fg_col)
        li_row = jnp.sum(jnp.where(eye, li_col, 0.0), axis=0, keepdims=True)
        lf_row = jnp.sum(jnp.where(eye, lf_col, 0.0), axis=0, keepdims=True)
        b_row = jnp.sum(jnp.where(row <= col, lf_col, 0.0), axis=0, keepdims=True)
        b_col = jnp.sum(jnp.where(causal, lf_row, 0.0), axis=1, keepdims=True)
        b_last = b_col[L - 1:L, :]
        d = jnp.where(causal, b_col - b_row + li_row, -jnp.inf)
        a_col = b_col + m_prev[h]
        mt = jnp.maximum(a_col, jnp.max(d, axis=1, keepdims=True))
        mn = mt[L - 1:L, :]
        m_t.append(mt)
        m_new.append(mn)
        w_intra.append(jnp.exp(d - mt))
        w_inter.append(jnp.exp(a_col - mt))
        w_end.append(jnp.exp(b_last - b_col + li_col - mn))
        decay.append(jnp.exp(b_last + m_prev[h] - mn))

    kw = [kf[h] * w_end[h] for h in heads]
    k_v = [_dot_tn(kw[h].astype(BF16), vb[h]) for h in heads]
    s = [s_raw[h] * w_intra[h] for h in heads]
    s_v = [_dot(s[h].astype(BF16), vb[h]) for h in heads]
    for h in heads:
        c_s[h] = decay[h] * c_prev[h] + k_v[h]
        n_s[h] = decay[h] * n_prev[h] + jnp.sum(kw[h], axis=0, keepdims=True)
        m_s[h] = m_new[h]
    for h in heads:
        num = w_inter[h] * q_c[h] + s_v[h]
        den = (w_inter[h] * jnp.sum(qf[h] * n_prev[h], axis=1, keepdims=True)
               + jnp.sum(s[h], axis=1, keepdims=True))
        hh = num / jnp.maximum(jnp.abs(den), jnp.exp(-m_t[h]))
        hn = hh * lax.rsqrt(jnp.mean(hh * hh, axis=1, keepdims=True) + RMS_EPS)
        hn = hn * wn_ref[:, dv[h]]
        og = jax.nn.sigmoid(o_ref[:, dv[h]])
        h_ref[:, dv[h]] = (og * hn).astype(h_ref.dtype)

    @pl.when(c == pl.num_programs(1) - 1)
    def _():
        cout_ref[0] = c_s[...]
        nout_ref[0] = n_s[...]
        mout_ref[0] = m_s[...]


def _mlstm(p, gates, bg, wn, c0, n0, m0, *, batch, L, out_dtype):
    t_total = p.shape[0]
    nc = t_total // (batch * L)
    row = lambda b, c: b * nc + c
    return pl.pallas_call(
        functools.partial(_mlstm_kernel, L=L),
        grid=(batch, nc),
        in_specs=[
            pl.BlockSpec((L, H_M * DK_M), lambda b, c: (row(b, c), 0)),
            pl.BlockSpec((L, H_M * DK_M), lambda b, c: (row(b, c), 1)),
            pl.BlockSpec((L, W_M), lambda b, c: (row(b, c), 1)),
            pl.BlockSpec((L, W_M), lambda b, c: (row(b, c), 2)),
            pl.BlockSpec((L, LANES), lambda b, c: (row(b, c), 0)),
            pl.BlockSpec((1, LANES), lambda b, c: (0, 0)),
            pl.BlockSpec((1, W_M), lambda b, c: (0, 0)),
            pl.BlockSpec((1, H_M, DK_M, DV_M), lambda b, c: (b, 0, 0, 0)),
            pl.BlockSpec((1, H_M, 1, DK_M), lambda b, c: (b, 0, 0, 0)),
            pl.BlockSpec((1, H_M, 1, 1), lambda b, c: (b, 0, 0, 0)),
        ],
        out_specs=[
            pl.BlockSpec((L, W_M), lambda b, c: (row(b, c), 0)),
            pl.BlockSpec((1, H_M, DK_M, DV_M), lambda b, c: (b, 0, 0, 0)),
            pl.BlockSpec((1, H_M, 1, DK_M), lambda b, c: (b, 0, 0, 0)),
            pl.BlockSpec((1, H_M, 1, 1), lambda b, c: (b, 0, 0, 0)),
        ],
        out_shape=[
            jax.ShapeDtypeStruct((t_total, W_M), out_dtype),
            jax.ShapeDtypeStruct((batch, H_M, DK_M, DV_M), F32),
            jax.ShapeDtypeStruct((batch, H_M, 1, DK_M), F32),
            jax.ShapeDtypeStruct((batch, H_M, 1, 1), F32),
        ],
        scratch_shapes=[
            pltpu.VMEM((H_M, DK_M, DV_M), F32),
            pltpu.VMEM((H_M, 1, DK_M), F32),
            pltpu.VMEM((H_M, 1, 1), F32),
        ],
        compiler_params=_params(("parallel", "arbitrary"), 32),
        name="mlstm",
    )(p, p, p, p, gates, bg, wn, c0, n0, m0)


SBP_BQ = 512
SBP_BK = 256
SBP_NH = 4
SB_QSCALE = (DH_SB ** -0.5) * LOG2E


def _sb_prompt_kernel(bsb_ref, q_ref, k_ref, v_ref, o_ref, acc_ref, rest_ref,
                      u0_ref, u1_ref, t0_ref, t1_ref):
    hp = pl.program_id(0)
    i = pl.program_id(1)
    bq, bk = SBP_BQ, SBP_BK
    per_q = bq // bk
    nk = per_q * (i + 1)
    heads = [slice(hh * DH_SB, (hh + 1) * DH_SB) for hh in range(SBP_NH)]
    qs = [(q_ref[:, hs] * SB_QSCALE).astype(BF16) for hs in heads]
    bias = [bsb_ref[hp * SBP_NH + hh] * LOG2E for hh in range(SBP_NH)]
    later_aug = _later_aug(bk)
    row = lax.broadcasted_iota(jnp.int32, (bq, bk), 0)
    col = lax.broadcasted_iota(jnp.int32, (bq, bk), 1)
    acc_ref[...] = jnp.zeros_like(acc_ref)
    rest_ref[...] = jnp.zeros_like(rest_ref)

    def key_rows(jj):
        return pl.ds(pl.multiple_of((nk - 1 - jj) * bk, bk), bk)

    def scores(jj, u_ref, t_ref, diag):
        rows = key_rows(jj)
        mask = None if diag is None else (col + bk * (per_q - 1 - diag)) < row
        zs = [_dot_nt(qs[hh], k_ref[rows, hs]) + bias[hh] for hh, hs in enumerate(heads)]
        sps = [_softplus2(z) for z in zs]
        spms = sps if mask is None else [jnp.where(mask, sp, 0.0) for sp in sps]
        eas = [_dot(spm.astype(BF16), later_aug) for spm in spms]
        for hh in range(SBP_NH):
            u = zs[hh] - sps[hh] - eas[hh][:, :bk]
            u_ref[hh] = u if mask is None else jnp.where(mask, u, -jnp.inf)
            t_ref[hh] = eas[hh][:, bk:]

    def values(jj, u_ref, t_ref):
        rows = key_rows(jj)
        rests = [rest_ref[hh] for hh in range(SBP_NH)]
        a_s = [jnp.exp2(u_ref[hh] + jnp.concatenate([rests[hh]] * (bk // LANES), axis=1))
               for hh in range(SBP_NH)]
        outs = [_dot(a_s[hh].astype(BF16), v_ref[rows, hs]) for hh, hs in enumerate(heads)]
        for hh in range(SBP_NH):
            acc_ref[hh] += outs[hh]
            rest_ref[hh] = rests[hh] - t_ref[hh]

    scores(0, u0_ref, t0_ref, 0)
    scores(1, u1_ref, t1_ref, 1)
    values(0, u0_ref, t0_ref)

    def fused(jj, ua_ref, ta_ref, ub_ref, tb_ref):
        rows_a, rows_b = key_rows(jj), key_rows(jj - 1)
        nh = range(SBP_NH)
        zs = [_dot_nt(qs[hh], k_ref[rows_a, hs]) + bias[hh] for hh, hs in enumerate(heads)]
        rests = [rest_ref[hh] for hh in nh]
        a_s = [jnp.exp2(ub_ref[hh] + jnp.concatenate([rests[hh]] * (bk // LANES), axis=1)) for hh in nh]
        outs = [_dot(a_s[hh].astype(BF16), v_ref[rows_b, hs]) for hh, hs in enumerate(heads)]
        sps = [_softplus2(z) for z in zs]
        eas = [_dot(sp.astype(BF16), later_aug) for sp in sps]
        for hh in nh:
            acc_ref[hh] += outs[hh]
            rest_ref[hh] = rests[hh] - tb_ref[hh]
        for hh in nh:
            ua_ref[hh] = zs[hh] - sps[hh] - eas[hh][:, :bk]
            ta_ref[hh] = eas[hh][:, bk:]

    def pair(p, carry):
        jj = per_q * (p + 1)
        fused(jj, u0_ref, t0_ref, u1_ref, t1_ref)
        fused(jj + 1, u1_ref, t1_ref, u0_ref, t0_ref)
        return carry

    lax.fori_loop(0, i, pair, 0)
    values(nk - 1, u1_ref, t1_ref)
    for hh, hs in enumerate(heads):
        o_ref[:, hs] = acc_ref[hh].astype(o_ref.dtype)


def _sb_prompt(p, kb, vb, b_sb):
    assert SBP_BQ == 2 * SBP_BK
    T = p.shape[0]
    width = SBP_NH * DH_SB
    q_col0 = 3072 // width
    u_shape = pltpu.VMEM((SBP_NH, SBP_BQ, SBP_BK), F32)
    t_shape = pltpu.VMEM((SBP_NH, SBP_BQ, LANES), F32)
    return pl.pallas_call(
        _sb_prompt_kernel,
        grid=(H_SB // SBP_NH, T // SBP_BQ),
        in_specs=[
            pl.BlockSpec(memory_space=pltpu.SMEM),
            pl.BlockSpec((SBP_BQ, width), lambda h, i: (i, q_col0 + h)),
            pl.BlockSpec((T, width), lambda h, i: (0, h)),
            pl.BlockSpec((T, width), lambda h, i: (0, h)),
        ],
        out_specs=pl.BlockSpec((SBP_BQ, width), lambda h, i: (i, h)),
        out_shape=jax.ShapeDtypeStruct((T, W_SB), BF16),
        scratch_shapes=[pltpu.VMEM((SBP_NH, SBP_BQ, DH_SB), F32), t_shape,
                        u_shape, u_shape, t_shape, t_shape],
        compiler_params=_params(("parallel", "arbitrary"), 52),
        name="sb_prompt",
    )(b_sb, p, kb, vb)


SBS_PPS = 16


def _later_aug(n):
    row = lax.broadcasted_iota(jnp.int32, (n, n + LANES), 0)
    col = lax.broadcasted_iota(jnp.int32, (n, n + LANES), 1)
    return ((row > col) | (col >= n)).astype(BF16)


def _sb_sample_kernel(pt_ref, bsb_ref, q_ref, kn_ref, vn_ref, *rest, n_q):
    page_refs = rest[:2 * SBS_PPS]
    o_ref, acc_ref, rest_ref = rest[2 * SBS_PPS:]
    g = pl.program_id(1)
    rows = H_SB * n_q

    qs = [(q_ref[:, h * DH_SB:(h + 1) * DH_SB] * SB_QSCALE).astype(BF16) for h in range(H_SB)]
    bias = jnp.concatenate([jnp.full((n_q, PAGE_SIZE), bsb_ref[h] * LOG2E, F32) for h in range(H_SB)],
                           axis=0)
    later_aug = _later_aug(PAGE_SIZE)

    def head_dots_nt(ks):
        return jnp.concatenate([_dot_nt(qs[h], ks[h]) for h in range(H_SB)], axis=0)

    def head_dots(ab, vs):
        return jnp.concatenate([_dot(ab[h * n_q:(h + 1) * n_q, :], vs[h]) for h in range(H_SB)], axis=0)

    @pl.when(g == 0)
    def _():
        pad = jnp.zeros((PAGE_SIZE - n_q, DH_SB), BF16)
        ks = [jnp.concatenate([kn_ref[:, h * DH_SB:(h + 1) * DH_SB].astype(BF16), pad], axis=0)
              for h in range(H_SB)]
        vs = [jnp.concatenate([vn_ref[:, h * DH_SB:(h + 1) * DH_SB].astype(BF16), pad], axis=0)
              for h in range(H_SB)]
        r = lax.broadcasted_iota(jnp.int32, (rows, PAGE_SIZE), 0)
        cidx = lax.broadcasted_iota(jnp.int32, (rows, PAGE_SIZE), 1)
        mask = cidx < (r % n_q)
        z = head_dots_nt(ks) + bias
        sp = _softplus2(z)
        ea = _dot(jnp.where(mask, sp, 0.0).astype(BF16), later_aug)
        a = jnp.where(mask, jnp.exp2(z - sp - ea[:, :PAGE_SIZE]), 0.0)
        acc_ref[...] = head_dots(a.astype(BF16), vs)
        rest_ref[...] = -ea[:, PAGE_SIZE:]

    def head_rows(refs, h):
        return jnp.concatenate([r[pl.ds(h, PAGE_SIZE, stride=H_SB), :] for r in refs],
                               axis=0).astype(BF16)

    ks = [head_rows(page_refs[:SBS_PPS], h) for h in range(H_SB)]
    vs = [head_rows(page_refs[SBS_PPS:], h) for h in range(H_SB)]
    z = head_dots_nt(ks) + jnp.concatenate([bias] * SBS_PPS, axis=1)
    sp = _softplus2(z)
    spb = sp.astype(BF16)
    rest_v = rest_ref[...]
    es, rests = [], []
    for c in range(SBS_PPS):
        ea = _dot(spb[:, c * PAGE_SIZE:(c + 1) * PAGE_SIZE], later_aug)
        es.append(ea[:, :PAGE_SIZE])
        rests.append(rest_v)
        rest_v = rest_v - ea[:, PAGE_SIZE:]
    a = jnp.exp2(z - sp - jnp.concatenate(es, axis=1) + jnp.concatenate(rests, axis=1))
    acc = acc_ref[...] + head_dots(a.astype(BF16), vs)
    acc_ref[...] = acc
    rest_ref[...] = rest_v

    @pl.when(g == pl.num_programs(1) - 1)
    def _():
        for h in range(H_SB):
            o_ref[:, h * DH_SB:(h + 1) * DH_SB] = acc[h * n_q:(h + 1) * n_q, :]


def _sb_sample(p, k_new, v_new, cache_k, cache_v, page_table, b_sb, *, n_q):
    n_seq, n_pages = page_table.shape
    n_groups = n_pages // SBS_PPS

    def page_spec(c):
        return pl.BlockSpec(
            (PAGE_SIZE * H_SB, DH_SB),
            lambda b, g, pt: (pt[b, n_pages - 1 - (g * SBS_PPS + c)], 0))

    return pl.pallas_call(
        functools.partial(_sb_sample_kernel, n_q=n_q),
        grid_spec=pltpu.PrefetchScalarGridSpec(
            num_scalar_prefetch=1,
            grid=(n_seq, n_groups),
            in_specs=[
                pl.BlockSpec(memory_space=pltpu.SMEM),
                pl.BlockSpec((n_q, W_SB), lambda b, g, pt: (b, 3)),
                pl.BlockSpec((n_q, W_SB), lambda b, g, pt: (b, 0)),
                pl.BlockSpec((n_q, W_SB), lambda b, g, pt: (b, 0)),
            ] + [page_spec(c) for c in range(SBS_PPS)] * 2,
            out_specs=pl.BlockSpec((n_q, W_SB), lambda b, g, pt: (b, 0)),
            scratch_shapes=[pltpu.VMEM((H_SB * n_q, DH_SB), F32),
                            pltpu.VMEM((H_SB * n_q, LANES), F32)],
        ),
        out_shape=jax.ShapeDtypeStruct((n_seq * n_q, W_SB), F32),
        compiler_params=_params(("parallel", "arbitrary"), 48),
        name="sb_sample",
    )(page_table, b_sb, p, k_new, v_new,
      *([cache_k] * SBS_PPS), *([cache_v] * SBS_PPS))


ROW_CHUNK = 256


def _layer_norm(x, g, b):
    mu = jnp.mean(x, axis=-1, keepdims=True)
    xc = x - mu
    var = jnp.mean(xc * xc, axis=-1, keepdims=True)
    return xc * lax.rsqrt(var + LN_EPS) * g + b


def _outproj_kernel(hm_ref, hs_ref, x_ref, wm_ref, ws_ref, g_ref, b_ref, x1_ref, x1b_ref, *, tm):
    for c in range(tm // ROW_CHUNK):
        rows = slice(c * ROW_CHUNK, (c + 1) * ROW_CHUNK)
        mix = (_dot(hm_ref[rows, :].astype(BF16), wm_ref[...])
               + _dot(hs_ref[rows, :].astype(BF16), ws_ref[...]))
        x1 = _layer_norm(DN_ALPHA * x_ref[rows, :] + mix, g_ref[...], b_ref[...])
        x1_ref[rows, :] = x1
        x1b_ref[rows, :] = x1.astype(BF16)


def _outproj(hm, hs, x, w_out, ln_g, ln_b, tm):
    m = x.shape[0]
    return pl.pallas_call(
        functools.partial(_outproj_kernel, tm=tm),
        grid=(m // tm,),
        in_specs=[
            pl.BlockSpec((tm, W_M), lambda i: (i, 0)),
            pl.BlockSpec((tm, W_SB), lambda i: (i, 0)),
            pl.BlockSpec((tm, D_MODEL), lambda i: (i, 0)),
            pl.BlockSpec((W_M, D_MODEL), lambda i: (0, 0)),
            pl.BlockSpec((W_SB, D_MODEL), lambda i: (1, 0)),
            pl.BlockSpec((1, D_MODEL), lambda i: (0, 0)),
            pl.BlockSpec((1, D_MODEL), lambda i: (0, 0)),
        ],
        out_specs=[
            pl.BlockSpec((tm, D_MODEL), lambda i: (i, 0)),
            pl.BlockSpec((tm, D_MODEL), lambda i: (i, 0)),
        ],
        out_shape=[
            jax.ShapeDtypeStruct((m, D_MODEL), F32),
            jax.ShapeDtypeStruct((m, D_MODEL), BF16),
        ],
        compiler_params=_params(("parallel",), 48),
        name="outproj_ln",
    )(hm, hs, x, w_out, w_out, ln_g, ln_b)


UP_TN = 512
UP_NJ = D_FF_PAD // UP_TN
UP_CHUNK = 256
CARRY = 8


GELU_C1 = 0.7978845608028654
GELU_C3 = GELU_C1 * 0.044715


def _gelu_tanh(x):
    half = 0.5 * x
    return half + half * jnp.tanh(x * (GELU_C1 + GELU_C3 * (x * x)))


def _conv_taps(buf_ref, h, off, wc_ref, bc_ref):
    n = h.shape[0]
    buf_ref[CARRY + off:CARRY + off + n, :] = h
    h1 = buf_ref[CARRY + off - 1:CARRY + off - 1 + n, :]
    h2 = buf_ref[CARRY + off - 2:CARRY + off - 2 + n, :]
    return bc_ref[...] + wc_ref[0:1, :] * h2 + wc_ref[1:2, :] * h1 + wc_ref[2:3, :] * h


UP_SUB = UP_TN // LANES
UP_VALID = D_FF // LANES


def _up_weight_specs(tile_of):
    def spec(half, t):
        def index_map(*idx):
            return (0, half * UP_VALID + jnp.minimum(UP_SUB * tile_of(*idx) + t, UP_VALID - 1))
        return pl.BlockSpec((D_MODEL, LANES), index_map)
    return [spec(half, t) for half in range(2) for t in range(UP_SUB)]


def _cast_up_weights(j, w_refs, wu_s, wg_s):
    for t in range(UP_SUB):
        valid = UP_SUB * j + t < UP_VALID
        lanes = slice(t * LANES, (t + 1) * LANES)
        wu_s[:, lanes] = jnp.where(valid, w_refs[t][...], 0.0).astype(BF16)
        wg_s[:, lanes] = jnp.where(valid, w_refs[UP_SUB + t][...], 0.0).astype(BF16)


def _ffn_up_prompt_kernel(x_ref, *refs, tm):
    w_refs = refs[:2 * UP_SUB]
    (wcu_ref, wcg_ref, bcu_ref, bcg_ref, pu_ref, pg_ref,
     act_ref, cu_ref, cg_ref, bu_s, bg_s, wu_ref, wg_ref) = refs[2 * UP_SUB:]
    i = pl.program_id(1)

    @pl.when(i == 0)
    def _():
        _cast_up_weights(pl.program_id(0), w_refs, wu_ref, wg_ref)
        bu_s[CARRY - 2:CARRY, :] = pu_ref[...]
        bg_s[CARRY - 2:CARRY, :] = pg_ref[...]

    @pl.when(i > 0)
    def _():
        bu_s[0:CARRY, :] = bu_s[tm:tm + CARRY, :]
        bg_s[0:CARRY, :] = bg_s[tm:tm + CARRY, :]

    for c in range(tm // UP_CHUNK):
        off = c * UP_CHUNK
        x = x_ref[off:off + UP_CHUNK, :]
        hg = _dot(x, wg_ref[...])
        hu = _dot(x, wu_ref[...])
        gate = _gelu_tanh(_conv_taps(bg_s, hg, off, wcg_ref, bcg_ref))
        cu = _conv_taps(bu_s, hu, off, wcu_ref, bcu_ref)
        act_ref[off:off + UP_CHUNK, :] = (gate * cu).astype(act_ref.dtype)
    cu_ref[...] = hu[UP_CHUNK - 2:UP_CHUNK, :]
    cg_ref[...] = hg[UP_CHUNK - 2:UP_CHUNK, :]


def _ffn_up_prompt(x1b, w_up, wcu, wcg, bcu, bcg, pu, pg, tm):
    m = x1b.shape[0]
    col = lambda j, i: (0, j)
    w_scratch = pltpu.VMEM((D_MODEL, UP_TN), BF16)
    return pl.pallas_call(
        functools.partial(_ffn_up_prompt_kernel, tm=tm),
        grid=(UP_NJ, m // tm),
        in_specs=[pl.BlockSpec((tm, D_MODEL), lambda j, i: (i, 0))] + _up_weight_specs(lambda j, i: j) + [
            pl.BlockSpec((CONV_W, UP_TN), col),
            pl.BlockSpec((CONV_W, UP_TN), col),
            pl.BlockSpec((1, UP_TN), col),
            pl.BlockSpec((1, UP_TN), col),
            pl.BlockSpec((CONV_W - 1, UP_TN), col),
            pl.BlockSpec((CONV_W - 1, UP_TN), col),
        ],
        out_specs=[
            pl.BlockSpec((tm, UP_TN), lambda j, i: (i, j)),
            pl.BlockSpec((CONV_W - 1, UP_TN), col),
            pl.BlockSpec((CONV_W - 1, UP_TN), col),
        ],
        out_shape=[
            jax.ShapeDtypeStruct((m, D_FF_PAD), BF16),
            jax.ShapeDtypeStruct((CONV_W - 1, D_FF_PAD), F32),
            jax.ShapeDtypeStruct((CONV_W - 1, D_FF_PAD), F32),
        ],
        scratch_shapes=[pltpu.VMEM((tm + CARRY, UP_TN), F32), pltpu.VMEM((tm + CARRY, UP_TN), F32),
                        w_scratch, w_scratch],
        compiler_params=_params(("parallel", "arbitrary"), 44),
        name="ffn_up_prompt",
    )(x1b, *([w_up] * (2 * UP_SUB)), wcu, wcg, bcu, bcg, pu, pg)


def _conv_taps_seq(buf_ref, h, s0_ref, s1_ref, wc_ref, bc_ref, seg):
    n = h.shape[0]
    n_seq = n // seg
    h1s, h2s = [], []
    for s in range(h.shape[1] // LANES):
        lanes = slice(s * LANES, (s + 1) * LANES)
        buf_ref[s, CARRY:CARRY + n, :] = h[:, lanes]
        buf_ref[s, pl.ds(CARRY - 1, n_seq, stride=seg), :] = s1_ref[:, lanes]
        h1s.append(buf_ref[s, CARRY - 1:CARRY - 1 + n, :])
        buf_ref[s, pl.ds(CARRY - 2, n_seq, stride=seg), :] = s0_ref[:, lanes]
        h2s.append(buf_ref[s, CARRY - 2:CARRY - 2 + n, :])
    h1 = jnp.concatenate(h1s, axis=1)
    h2 = jnp.concatenate(h2s, axis=1)
    return bc_ref[...] + wc_ref[0:1, :] * h2 + wc_ref[1:2, :] * h1 + wc_ref[2:3, :] * h


def _ffn_up_sample_kernel(x_ref, *refs, seg):
    w_refs = refs[:2 * UP_SUB]
    (wcu_ref, wcg_ref, bcu_ref, bcg_ref, s0u_ref, s1u_ref, s0g_ref, s1g_ref,
     act_ref, hu_ref, hg_ref, bu_s, bg_s, wu_ref, wg_ref) = refs[2 * UP_SUB:]
    _cast_up_weights(pl.program_id(0), w_refs, wu_ref, wg_ref)
    x = x_ref[...]
    hu = _dot(x, wu_ref[...])
    hg = _dot(x, wg_ref[...])
    cu = _conv_taps_seq(bu_s, hu, s0u_ref, s1u_ref, wcu_ref, bcu_ref, seg)
    cg = _conv_taps_seq(bg_s, hg, s0g_ref, s1g_ref, wcg_ref, bcg_ref, seg)
    act_ref[...] = (_gelu_tanh(cg) * cu).astype(act_ref.dtype)
    hu_ref[...] = hu
    hg_ref[...] = hg


def _ffn_up_sample(x1b, w_up, wcu, wcg, bcu, bcg, s0u, s1u, s0g, s1g, seg):
    m = x1b.shape[0]
    n_seq = m // seg
    col = lambda j: (0, j)
    w_scratch = pltpu.VMEM((D_MODEL, UP_TN), BF16)
    c_spec = pl.BlockSpec((CONV_W, UP_TN), col)
    b_spec = pl.BlockSpec((1, UP_TN), col)
    s_spec = pl.BlockSpec((n_seq, UP_TN), col)
    o_spec = pl.BlockSpec((m, UP_TN), col)
    return pl.pallas_call(
        functools.partial(_ffn_up_sample_kernel, seg=seg),
        grid=(UP_NJ,),
        in_specs=[pl.BlockSpec((m, D_MODEL), lambda j: (0, 0))] + _up_weight_specs(lambda j: j) + [
            c_spec, c_spec, b_spec, b_spec, s_spec, s_spec, s_spec, s_spec],
        out_specs=[o_spec, o_spec, o_spec],
        out_shape=[
            jax.ShapeDtypeStruct((m, D_FF_PAD), BF16),
            jax.ShapeDtypeStruct((m, D_FF_PAD), F32),
            jax.ShapeDtypeStruct((m, D_FF_PAD), F32),
        ],
        scratch_shapes=[pltpu.VMEM((UP_TN // LANES, m + CARRY, LANES), F32)] * 2 + [w_scratch, w_scratch],
        compiler_params=_params(("parallel",), 36),
        name="ffn_up_sample",
    )(x1b, *([w_up] * (2 * UP_SUB)), wcu, wcg, bcu, bcg, s0u, s1u, s0g, s1g)


DOWN_TK = 1408
DOWN_NK = D_FF_PAD // DOWN_TK
DOWN_LAST = D_FF - (DOWN_NK - 1) * DOWN_TK


def _ffn_down_kernel(a_ref, w_ref, x1_ref, g_ref, b_ref, y_ref, acc_ref):
    k = pl.program_id(1)

    @pl.when(k == 0)
    def _():
        acc_ref[...] = _dot(a_ref[...], w_ref[...])

    @pl.when((k > 0) & (k < DOWN_NK - 1))
    def _():
        acc_ref[...] += _dot(a_ref[...], w_ref[...])

    @pl.when(k == DOWN_NK - 1)
    def _():
        ffn = acc_ref[...] + _dot(a_ref[:, :DOWN_LAST], w_ref[:DOWN_LAST, :])
        y_ref[...] = _layer_norm(DN_ALPHA * x1_ref[...] + ffn, g_ref[...], b_ref[...])


def _ffn_down(act, w_down, x1, ln_g, ln_b, tm):
    m = act.shape[0]
    return pl.pallas_call(
        _ffn_down_kernel,
        grid=(m // tm, DOWN_NK),
        in_specs=[
            pl.BlockSpec((tm, DOWN_TK), lambda i, k: (i, k)),
            pl.BlockSpec((DOWN_TK, D_MODEL), lambda i, k: (k, 0)),
            pl.BlockSpec((tm, D_MODEL), lambda i, k: (i, 0)),
            pl.BlockSpec((1, D_MODEL), lambda i, k: (0, 0)),
            pl.BlockSpec((1, D_MODEL), lambda i, k: (0, 0)),
        ],
        out_specs=pl.BlockSpec((tm, D_MODEL), lambda i, k: (i, 0)),
        out_shape=jax.ShapeDtypeStruct((m, D_MODEL), F32),
        scratch_shapes=[pltpu.VMEM((tm, D_MODEL), F32)],
        compiler_params=_params(("parallel", "arbitrary"), 48),
        name="ffn_down_ln",
    )(act, w_down, x1, ln_g, ln_b)


def _pad_cols(a, width):
    return jnp.pad(a, ((0, 0), (0, width - a.shape[1])))


def _halves(a, dtype=F32):
    return (_pad_cols(a[:, :D_FF], D_FF_PAD).astype(dtype), _pad_cols(a[:, D_FF:], D_FF_PAD).astype(dtype))


def _unsplit(u, g):
    return jnp.concatenate([u[..., :D_FF], g[..., :D_FF]], axis=-1)


def kernel(x_prompt, x_sample, cache_k, cache_v, state_C, state_n, state_m, state_conv, page_table,
           w_in, b_gates, b_sb, w_mlstm_norm, w_out, ln1_g, ln1_b, w_up, w_conv, b_conv, w_down,
           ln2_g, ln2_b):
    n_b, seq, _ = x_prompt.shape
    n_s, n_q, _ = x_sample.shape
    assert n_b == 1

    wi = w_in[0]
    wi_t = wi.T
    w_a = wi_t[:GATE_COL].astype(BF16)
    w_b = wi_t[GATE_COL + N_GATES:].astype(BF16)
    w_gate = jnp.pad(wi_t[GATE_COL:GATE_COL + N_GATES], ((0, LANES - N_GATES), (0, 0))).astype(BF16)
    bg = _pad_cols(b_gates[0][None, :], LANES)
    wn = w_mlstm_norm[0][None, :]
    wo = w_out[0].astype(BF16)
    wcu, wcg = _halves(w_conv[0])
    bcu, bcg = _halves(b_conv[0][None, :])
    wd = w_down[0].astype(BF16)
    g1, b1 = ln1_g[0][None, :], ln1_b[0][None, :]
    g2, b2 = ln2_g[0][None, :], ln2_b[0][None, :]
    bsb = b_sb[0]

    xp = x_prompt.reshape(seq, D_MODEL)
    p_p, k_p, v_p, kb_p, vb_p, gt_p = _inproj(xp, w_a, w_b, w_gate, 1024)
    hm_p, c_p, nn_p, mm_p = _mlstm(
        p_p, gt_p, bg, wn,
        jnp.zeros((1, H_M, DK_M, DV_M), F32), jnp.zeros((1, H_M, 1, DK_M), F32),
        jnp.zeros((1, H_M, 1, 1), F32), batch=1, L=256, out_dtype=BF16)
    hs_p = _sb_prompt(p_p, kb_p, vb_p, bsb)
    x1_p, x1b_p = _outproj(hm_p, hs_p, xp, wo, g1, b1, 512)
    conv0 = jnp.zeros((CONV_W - 1, D_FF_PAD), F32)
    act_p, cu_p, cg_p = _ffn_up_prompt(x1b_p, w_up[0], wcu, wcg, bcu, bcg, conv0, conv0, 1024)
    y_p = _ffn_down(act_p, wd, x1_p, g2, b2, 512)

    xs = x_sample.reshape(n_s * n_q, D_MODEL)
    p_s, k_s, v_s, _, _, gt_s = _inproj(xs, w_a, w_b, w_gate, n_s * n_q)
    hm_s, c_s, nn_s, mm_s = _mlstm(
        p_s, gt_s, bg, wn, state_C[0], state_n[0][:, :, None, :], state_m[0][:, :, None, None],
        batch=n_s, L=n_q, out_dtype=F32)
    hs_s = _sb_sample(p_s, k_s, v_s, cache_k.reshape(-1, DH_SB), cache_v.reshape(-1, DH_SB),
                      page_table, bsb, n_q=n_q)
    x1_s, x1b_s = _outproj(hm_s, hs_s, xs, wo, g1, b1, n_s * n_q)
    s0u, s0g = _halves(state_conv[0][:, 0, :])
    s1u, s1g = _halves(state_conv[0][:, 1, :])
    act_s, hu_s, hg_s = _ffn_up_sample(x1b_s, w_up[0], wcu, wcg, bcu, bcg, s0u, s1u, s0g, s1g, n_q)
    y_s = _ffn_down(act_s, wd, x1_s, g2, b2, n_s * n_q)

    conv_s = _unsplit(hu_s.reshape(n_s, n_q, D_FF_PAD)[:, n_q - 2:], hg_s.reshape(n_s, n_q, D_FF_PAD)[:, n_q - 2:])
    return (
        y_p.reshape(1, seq, D_MODEL),
        y_s.reshape(n_s, n_q, D_MODEL),
        k_p.reshape(1, 1, seq, H_SB, DH_SB),
        v_p.reshape(1, 1, seq, H_SB, DH_SB),
        c_p[None],
        nn_p.reshape(1, 1, H_M, DK_M),
        mm_p.reshape(1, 1, H_M),
        _unsplit(cu_p, cg_p)[None, None],
        k_s.reshape(1, n_s, n_q, H_SB, DH_SB),
        v_s.reshape(1, n_s, n_q, H_SB, DH_SB),
        c_s[None],
        nn_s.reshape(1, n_s, H_M, DK_M),
        mm_s.reshape(1, n_s, H_M),
        conv_s[None],
    )
```

```python
import functools

import jax
import jax.numpy as jnp
from jax import lax
from jax.experimental import pallas as pl
from jax.experimental.pallas import tpu as pltpu

F32 = jnp.float32
BF16 = jnp.bfloat16

D_MODEL = 2048
H_M = 4
DK_M = 128
DV_M = 256
W_M = H_M * DV_M
H_SB = 8
DH_SB = 128
W_SB = H_SB * DH_SB
D_FF = 5504
D_FF_PAD = 5632
CONV_W = 3
PAGE_SIZE = 128
LN_EPS = 1e-5
RMS_EPS = 1e-6
DN_ALPHA = 2.0 ** 0.25
GATE_COL = 3072
N_GATES = 2 * H_M
LANES = 128
MIB = 1024 * 1024
LOG2E = 1.4426950408889634


def _params(semantics, vmem_mib, flags=None):
    return pltpu.CompilerParams(dimension_semantics=semantics,
                                vmem_limit_bytes=int(vmem_mib * MIB), flags=flags)


def _dot(a, b):
    return jnp.dot(a, b, preferred_element_type=F32)


def _dot_nt(a, b):
    return lax.dot_general(a, b, (((1,), (1,)), ((), ())), preferred_element_type=F32)


def _dot_tn(a, b):
    return lax.dot_general(a, b, (((0,), (0,)), ((), ())), preferred_element_type=F32)


def _softplus(z):
    return jnp.maximum(z, 0.0) + jnp.log1p(jnp.exp(-jnp.abs(z)))


def _softplus2(z):
    return jnp.maximum(z, 0.0) + jnp.log2(1.0 + jnp.exp2(-jnp.abs(z)))


IN_TN = 512
IN_NJ_A = GATE_COL // IN_TN
IN_NJ_P = 4096 // IN_TN
IN_NJ = IN_NJ_P + 4


def _inproj_kernel(x_ref, wa_ref, wb_ref, wg_ref, p_ref, k_ref, v_ref, kb_ref, vb_ref, g_ref, xb_ref):
    j = pl.program_id(1)

    @pl.when(j == 0)
    def _():
        xb = x_ref[...].astype(BF16)
        xb_ref[...] = xb
        g_ref[...] = _dot_nt(xb, wg_ref[...])

    @pl.when(j < IN_NJ_A)
    def _():
        p_ref[...] = _dot_nt(xb_ref[...], wa_ref[...])

    @pl.when(j >= IN_NJ_A)
    def _():
        acc = _dot_nt(xb_ref[...], wb_ref[...])

        @pl.when(j < IN_NJ_P)
        def _():
            p_ref[...] = acc

        @pl.when((j >= IN_NJ_P) & (j < IN_NJ_P + 2))
        def _():
            k_ref[...] = acc
            kb_ref[...] = acc.astype(BF16)

        @pl.when(j >= IN_NJ_P + 2)
        def _():
            v_ref[...] = acc
            vb_ref[...] = acc.astype(BF16)


def _inproj(x, w_a, w_b, w_gate, tm):
    m = x.shape[0]
    k_blk = lambda i, j: (i, jnp.clip(j - IN_NJ_P, 0, 1))
    v_blk = lambda i, j: (i, jnp.clip(j - IN_NJ_P - 2, 0, 1))
    return pl.pallas_call(
        _inproj_kernel,
        grid=(m // tm, IN_NJ),
        in_specs=[
            pl.BlockSpec((tm, D_MODEL), lambda i, j: (i, 0)),
            pl.BlockSpec((IN_TN, D_MODEL), lambda i, j: (jnp.minimum(j, IN_NJ_A - 1), 0)),
            pl.BlockSpec((IN_TN, D_MODEL), lambda i, j: (jnp.maximum(j - IN_NJ_A, 0), 0)),
            pl.BlockSpec((LANES, D_MODEL), lambda i, j: (0, 0)),
        ],
        out_specs=[
            pl.BlockSpec((tm, IN_TN), lambda i, j: (i, jnp.minimum(j, IN_NJ_P - 1))),
            pl.BlockSpec((tm, IN_TN), k_blk),
            pl.BlockSpec((tm, IN_TN), v_blk),
            pl.BlockSpec((tm, IN_TN), k_blk),
            pl.BlockSpec((tm, IN_TN), v_blk),
            pl.BlockSpec((tm, LANES), lambda i, j: (i, 0)),
        ],
        out_shape=[
            jax.ShapeDtypeStruct((m, 4096), F32),
            jax.ShapeDtypeStruct((m, W_SB), F32),
            jax.ShapeDtypeStruct((m, W_SB), F32),
            jax.ShapeDtypeStruct((m, W_SB), BF16),
            jax.ShapeDtypeStruct((m, W_SB), BF16),
            jax.ShapeDtypeStruct((m, LANES), F32),
        ],
        scratch_shapes=[pltpu.VMEM((tm, D_MODEL), BF16)],
        compiler_params=_params(("parallel", "arbitrary"), 54),
        name="inproj",
    )(x, w_a, w_b, w_gate)


def _mlstm_kernel(q_ref, k_ref, v_ref, o_ref, g_ref, bg_ref, wn_ref, c0_ref, n0_ref, m0_ref,
                  h_ref, cout_ref, nout_ref, mout_ref, c_s, n_s, m_s, *, L):
    c = pl.program_id(1)

    @pl.when(c == 0)
    def _():
        c_s[...] = c0_ref[0]
        n_s[...] = n0_ref[0]
        m_s[...] = m0_ref[0]

    g = g_ref[...] + bg_ref[...]
    row = lax.broadcasted_iota(jnp.int32, (L, L), 0)
    col = lax.broadcasted_iota(jnp.int32, (L, L), 1)
    causal = col <= row
    eye = col == row
    heads = range(H_M)
    dk = [slice(h * DK_M, (h + 1) * DK_M) for h in heads]
    dv = [slice(h * DV_M, (h + 1) * DV_M) for h in heads]
    qf = [q_ref[:, dk[h]] * (DK_M ** -0.5) for h in heads]
    kf = [k_ref[:, dk[h]] for h in heads]
    qb = [x.astype(BF16) for x in qf]
    kb = [x.astype(BF16) for x in kf]
    vb = [v_ref[:, dv[h]].astype(BF16) for h in heads]
    c_prev = [c_s[h] for h in heads]
    n_prev = [n_s[h] for h in heads]
    m_prev = [m_s[h] for h in heads]
    s_raw = [_dot_nt(qb[h], kb[h]) for h in heads]
    q_c = [_dot(qb[h], c_prev[h].astype(BF16)) for h in heads]

    w_intra, w_inter, w_end, decay, m_t, m_new = [], [], [], [], [], []
    for h in heads:
        li_col = g[:, h:h + 1]
        fg_col = g[:, H_M + h:H_M + h + 1]
        lf_col = -_softplus(-fg_col)
        li_row = jnp.sum(jnp.where(eye, li_col, 0.0), axis=0, keepdims=True)
        lf_row = jnp.sum(jnp.where(eye, lf_col, 0.0), axis=0, keepdims=True)
        b_row = jnp.sum(jnp.where(row <= col, lf_col, 0.0), axis=0, keepdims=True)
        b_col = jnp.sum(jnp.where(causal, lf_row, 0.0), axis=1, keepdims=True)
        b_last = b_col[L - 1:L, :]
        d = jnp.where(causal, b_col - b_row + li_row, -jnp.inf)
        a_col = b_col + m_prev[h]
        mt = jnp.maximum(a_col, jnp.max(d, axis=1, keepdims=True))
        mn = mt[L - 1:L, :]
        m_t.append(mt)
        m_new.append(mn)
        w_intra.append(jnp.exp(d - mt))
        w_inter.append(jnp.exp(a_col - mt))
        w_end.append(jnp.exp(b_last - b_col + li_col - mn))
        decay.append(jnp.exp(b_last + m_prev[h] - mn))

    kw = [kf[h] * w_end[h] for h in heads]
    k_v = [_dot_tn(kw[h].astype(BF16), vb[h]) for h in heads]
    s = [s_raw[h] * w_intra[h] for h in heads]
    s_v = [_dot(s[h].astype(BF16), vb[h]) for h in heads]
    for h in heads:
        c_s[h] = decay[h] * c_prev[h] + k_v[h]
        n_s[h] = decay[h] * n_prev[h] + jnp.sum(kw[h], axis=0, keepdims=True)
        m_s[h] = m_new[h]
    for h in heads:
        num = w_inter[h] * q_c[h] + s_v[h]
        den = (w_inter[h] * jnp.sum(qf[h] * n_prev[h], axis=1, keepdims=True)
               + jnp.sum(s[h], axis=1, keepdims=True))
        hh = num / jnp.maximum(jnp.abs(den), jnp.exp(-m_t[h]))
        hn = hh * lax.rsqrt(jnp.mean(hh * hh, axis=1, keepdims=True) + RMS_EPS)
        hn = hn * wn_ref[:, dv[h]]
        og = jax.nn.sigmoid(o_ref[:, dv[h]])
        h_ref[:, dv[h]] = (og * hn).astype(h_ref.dtype)

    @pl.when(c == pl.num_programs(1) - 1)
    def _():
        cout_ref[0] = c_s[...]
        nout_ref[0] = n_s[...]
        mout_ref[0] = m_s[...]


def _mlstm(p, gates, bg, wn, c0, n0, m0, *, batch, L, out_dtype):
    t_total = p.shape[0]
    nc = t_total // (batch * L)
    row = lambda b, c: b * nc + c
    return pl.pallas_call(
        functools.partial(_mlstm_kernel, L=L),
        grid=(batch, nc),
        in_specs=[
            pl.BlockSpec((L, H_M * DK_M), lambda b, c: (row(b, c), 0)),
            pl.BlockSpec((L, H_M * DK_M), lambda b, c: (row(b, c), 1)),
            pl.BlockSpec((L, W_M), lambda b, c: (row(b, c), 1)),
            pl.BlockSpec((L, W_M), lambda b, c: (row(b, c), 2)),
            pl.BlockSpec((L, LANES), lambda b, c: (row(b, c), 0)),
            pl.BlockSpec((1, LANES), lambda b, c: (0, 0)),
            pl.BlockSpec((1, W_M), lambda b, c: (0, 0)),
            pl.BlockSpec((1, H_M, DK_M, DV_M), lambda b, c: (b, 0, 0, 0)),
            pl.BlockSpec((1, H_M, 1, DK_M), lambda b, c: (b, 0, 0, 0)),
            pl.BlockSpec((1, H_M, 1, 1), lambda b, c: (b, 0, 0, 0)),
        ],
        out_specs=[
            pl.BlockSpec((L, W_M), lambda b, c: (row(b, c), 0)),
            pl.BlockSpec((1, H_M, DK_M, DV_M), lambda b, c: (b, 0, 0, 0)),
            pl.BlockSpec((1, H_M, 1, DK_M), lambda b, c: (b, 0, 0, 0)),
            pl.BlockSpec((1, H_M, 1, 1), lambda b, c: (b, 0, 0, 0)),
        ],
        out_shape=[
            jax.ShapeDtypeStruct((t_total, W_M), out_dtype),
            jax.ShapeDtypeStruct((batch, H_M, DK_M, DV_M), F32),
            jax.ShapeDtypeStruct((batch, H_M, 1, DK_M), F32),
            jax.ShapeDtypeStruct((batch, H_M, 1, 1), F32),
        ],
        scratch_shapes=[
            pltpu.VMEM((H_M, DK_M, DV_M), F32),
            pltpu.VMEM((H_M, 1, DK_M), F32),
            pltpu.VMEM((H_M, 1, 1), F32),
        ],
        compiler_params=_params(("parallel", "arbitrary"), 32),
        name="mlstm",
    )(p, p, p, p, gates, bg, wn, c0, n0, m0)


SBP_BQ = 512
SBP_BK = 256
SBP_NH = 4
SB_QSCALE = (DH_SB ** -0.5) * LOG2E


def _sb_prompt_kernel(bsb_ref, q_ref, k_ref, v_ref, o_ref, acc_ref, rest_ref,
                      u0_ref, u1_ref, t0_ref, t1_ref):
    hp = pl.program_id(0)
    i = pl.program_id(1)
    bq, bk = SBP_BQ, SBP_BK
    per_q = bq // bk
    nk = per_q * (i + 1)
    heads = [slice(hh * DH_SB, (hh + 1) * DH_SB) for hh in range(SBP_NH)]
    qs = [(q_ref[:, hs] * SB_QSCALE).astype(BF16) for hs in heads]
    bias = [bsb_ref[hp * SBP_NH + hh] * LOG2E for hh in range(SBP_NH)]
    later_aug = _later_aug(bk)
    row = lax.broadcasted_iota(jnp.int32, (bq, bk), 0)
    col = lax.broadcasted_iota(jnp.int32, (bq, bk), 1)
    acc_ref[...] = jnp.zeros_like(acc_ref)
    rest_ref[...] = jnp.zeros_like(rest_ref)

    def key_rows(jj):
        return pl.ds(pl.multiple_of((nk - 1 - jj) * bk, bk), bk)

    def scores(jj, u_ref, t_ref, diag):
        rows = key_rows(jj)
        mask = None if diag is None else (col + bk * (per_q - 1 - diag)) < row
        zs = [_dot_nt(qs[hh], k_ref[rows, hs]) + bias[hh] for hh, hs in enumerate(heads)]
        sps = [_softplus2(z) for z in zs]
        spms = sps if mask is None else [jnp.where(mask, sp, 0.0) for sp in sps]
        eas = [_dot(spm.astype(BF16), later_aug) for spm in spms]
        for hh in range(SBP_NH):
            u = zs[hh] - sps[hh] - eas[hh][:, :bk]
            u_ref[hh] = u if mask is None else jnp.where(mask, u, -jnp.inf)
            t_ref[hh] = eas[hh][:, bk:]

    def values(jj, u_ref, t_ref):
        rows = key_rows(jj)
        rests = [rest_ref[hh] for hh in range(SBP_NH)]
        a_s = [jnp.exp2(u_ref[hh] + jnp.concatenate([rests[hh]] * (bk // LANES), axis=1))
               for hh in range(SBP_NH)]
        outs = [_dot(a_s[hh].astype(BF16), v_ref[rows, hs]) for hh, hs in enumerate(heads)]
        for hh in range(SBP_NH):
            acc_ref[hh] += outs[hh]
            rest_ref[hh] = rests[hh] - t_ref[hh]

    scores(0, u0_ref, t0_ref, 0)
    scores(1, u1_ref, t1_ref, 1)
    values(0, u0_ref, t0_ref)

    def fused(jj, ua_ref, ta_ref, ub_ref, tb_ref):
        rows_a, rows_b = key_rows(jj), key_rows(jj - 1)
        nh = range(SBP_NH)
        zs = [_dot_nt(qs[hh], k_ref[rows_a, hs]) + bias[hh] for hh, hs in enumerate(heads)]
        rests = [rest_ref[hh] for hh in nh]
        a_s = [jnp.exp2(ub_ref[hh] + jnp.concatenate([rests[hh]] * (bk // LANES), axis=1)) for hh in nh]
        outs = [_dot(a_s[hh].astype(BF16), v_ref[rows_b, hs]) for hh, hs in enumerate(heads)]
        sps = [_softplus2(z) for z in zs]
        eas = [_dot(sp.astype(BF16), later_aug) for sp in sps]
        for hh in nh:
            acc_ref[hh] += outs[hh]
            rest_ref[hh] = rests[hh] - tb_ref[hh]
        for hh in nh:
            ua_ref[hh] = zs[hh] - sps[hh] - eas[hh][:, :bk]
            ta_ref[hh] = eas[hh][:, bk:]

    def pair(p, carry):
        jj = per_q * (p + 1)
        fused(jj, u0_ref, t0_ref, u1_ref, t1_ref)
        fused(jj + 1, u1_ref, t1_ref, u0_ref, t0_ref)
        return carry

    lax.fori_loop(0, i, pair, 0)
    values(nk - 1, u1_ref, t1_ref)
    for hh, hs in enumerate(heads):
        o_ref[:, hs] = acc_ref[hh].astype(o_ref.dtype)


def _sb_prompt(p, kb, vb, b_sb):
    assert SBP_BQ == 2 * SBP_BK
    T = p.shape[0]
    width = SBP_NH * DH_SB
    q_col0 = 3072 // width
    u_shape = pltpu.VMEM((SBP_NH, SBP_BQ, SBP_BK), F32)
    t_shape = pltpu.VMEM((SBP_NH, SBP_BQ, LANES), F32)
    return pl.pallas_call(
        _sb_prompt_kernel,
        grid=(H_SB // SBP_NH, T // SBP_BQ),
        in_specs=[
            pl.BlockSpec(memory_space=pltpu.SMEM),
            pl.BlockSpec((SBP_BQ, width), lambda h, i: (i, q_col0 + h)),
            pl.BlockSpec((T, width), lambda h, i: (0, h)),
            pl.BlockSpec((T, width), lambda h, i: (0, h)),
        ],
        out_specs=pl.BlockSpec((SBP_BQ, width), lambda h, i: (i, h)),
        out_shape=jax.ShapeDtypeStruct((T, W_SB), BF16),
        scratch_shapes=[pltpu.VMEM((SBP_NH, SBP_BQ, DH_SB), F32), t_shape,
                        u_shape, u_shape, t_shape, t_shape],
        compiler_params=_params(("parallel", "arbitrary"), 52),
        name="sb_prompt",
    )(b_sb, p, kb, vb)


SBS_PPS = 16


def _later_aug(n):
    row = lax.broadcasted_iota(jnp.int32, (n, n + LANES), 0)
    col = lax.broadcasted_iota(jnp.int32, (n, n + LANES), 1)
    return ((row > col) | (col >= n)).astype(BF16)


def _sb_sample_kernel(pt_ref, bsb_ref, q_ref, kn_ref, vn_ref, *rest, n_q):
    page_refs = rest[:2 * SBS_PPS]
    o_ref, acc_ref, rest_ref = rest[2 * SBS_PPS:]
    g = pl.program_id(1)
    rows = H_SB * n_q

    qs = [(q_ref[:, h * DH_SB:(h + 1) * DH_SB] * SB_QSCALE).astype(BF16) for h in range(H_SB)]
    bias = jnp.concatenate([jnp.full((n_q, PAGE_SIZE), bsb_ref[h] * LOG2E, F32) for h in range(H_SB)],
                           axis=0)
    later_aug = _later_aug(PAGE_SIZE)

    def head_dots_nt(ks):
        return jnp.concatenate([_dot_nt(qs[h], ks[h]) for h in range(H_SB)], axis=0)

    def head_dots(ab, vs):
        return jnp.concatenate([_dot(ab[h * n_q:(h + 1) * n_q, :], vs[h]) for h in range(H_SB)], axis=0)

    @pl.when(g == 0)
    def _():
        pad = jnp.zeros((PAGE_SIZE - n_q, DH_SB), BF16)
        ks = [jnp.concatenate([kn_ref[:, h * DH_SB:(h + 1) * DH_SB].astype(BF16), pad], axis=0)
              for h in range(H_SB)]
        vs = [jnp.concatenate([vn_ref[:, h * DH_SB:(h + 1) * DH_SB].astype(BF16), pad], axis=0)
              for h in range(H_SB)]
        r = lax.broadcasted_iota(jnp.int32, (rows, PAGE_SIZE), 0)
        cidx = lax.broadcasted_iota(jnp.int32, (rows, PAGE_SIZE), 1)
        mask = cidx < (r % n_q)
        z = head_dots_nt(ks) + bias
        sp = _softplus2(z)
        ea = _dot(jnp.where(mask, sp, 0.0).astype(BF16), later_aug)
        a = jnp.where(mask, jnp.exp2(z - sp - ea[:, :PAGE_SIZE]), 0.0)
        acc_ref[...] = head_dots(a.astype(BF16), vs)
        rest_ref[...] = -ea[:, PAGE_SIZE:]

    def head_rows(refs, h):
        return jnp.concatenate([r[pl.ds(h, PAGE_SIZE, stride=H_SB), :] for r in refs],
                               axis=0).astype(BF16)

    ks = [head_rows(page_refs[:SBS_PPS], h) for h in range(H_SB)]
    vs = [head_rows(page_refs[SBS_PPS:], h) for h in range(H_SB)]
    z = head_dots_nt(ks) + jnp.concatenate([bias] * SBS_PPS, axis=1)
    sp = _softplus2(z)
    spb = sp.astype(BF16)
    rest_v = rest_ref[...]
    es, rests = [], []
    for c in range(SBS_PPS):
        ea = _dot(spb[:, c * PAGE_SIZE:(c + 1) * PAGE_SIZE], later_aug)
        es.append(ea[:, :PAGE_SIZE])
        rests.append(rest_v)
        rest_v = rest_v - ea[:, PAGE_SIZE:]
    a = jnp.exp2(z - sp - jnp.concatenate(es, axis=1) + jnp.concatenate(rests, axis=1))
    acc = acc_ref[...] + head_dots(a.astype(BF16), vs)
    acc_ref[...] = acc
    rest_ref[...] = rest_v

    @pl.when(g == pl.num_programs(1) - 1)
    def _():
        for h in range(H_SB):
            o_ref[:, h * DH_SB:(h + 1) * DH_SB] = acc[h * n_q:(h + 1) * n_q, :]


def _sb_sample(p, k_new, v_new, cache_k, cache_v, page_table, b_sb, *, n_q):
    n_seq, n_pages = page_table.shape
    n_groups = n_pages // SBS_PPS

    def page_spec(c):
        return pl.BlockSpec(
            (PAGE_SIZE * H_SB, DH_SB),
            lambda b, g, pt: (pt[b, n_pages - 1 - (g * SBS_PPS + c)], 0))

    return pl.pallas_call(
        functools.partial(_sb_sample_kernel, n_q=n_q),
        grid_spec=pltpu.PrefetchScalarGridSpec(
            num_scalar_prefetch=1,
            grid=(n_seq, n_groups),
            in_specs=[
                pl.BlockSpec(memory_space=pltpu.SMEM),
                pl.BlockSpec((n_q, W_SB), lambda b, g, pt: (b, 3)),
                pl.BlockSpec((n_q, W_SB), lambda b, g, pt: (b, 0)),
                pl.BlockSpec((n_q, W_SB), lambda b, g, pt: (b, 0)),
            ] + [page_spec(c) for c in range(SBS_PPS)] * 2,
            out_specs=pl.BlockSpec((n_q, W_SB), lambda b, g, pt: (b, 0)),
            scratch_shapes=[pltpu.VMEM((H_SB * n_q, DH_SB), F32),
                            pltpu.VMEM((H_SB * n_q, LANES), F32)],
        ),
        out_shape=jax.ShapeDtypeStruct((n_seq * n_q, W_SB), F32),
        compiler_params=_params(("parallel", "arbitrary"), 48),
        name="sb_sample",
    )(page_table, b_sb, p, k_new, v_new,
      *([cache_k] * SBS_PPS), *([cache_v] * SBS_PPS))


ROW_CHUNK = 256


def _layer_norm(x, g, b):
    mu = jnp.mean(x, axis=-1, keepdims=True)
    xc = x - mu
    var = jnp.mean(xc * xc, axis=-1, keepdims=True)
    return xc * lax.rsqrt(var + LN_EPS) * g + b


def _outproj_kernel(hm_ref, hs_ref, x_ref, w_ref, g_ref, b_ref, x1_ref, x1b_ref, *, tm):
    for c in range(tm // ROW_CHUNK):
        rows = slice(c * ROW_CHUNK, (c + 1) * ROW_CHUNK)
        h = jnp.concatenate([hm_ref[rows, :].astype(BF16), hs_ref[rows, :].astype(BF16)], axis=1)
        mix = _dot(h, w_ref[...])
        x1 = _layer_norm(DN_ALPHA * x_ref[rows, :] + mix, g_ref[...], b_ref[...])
        x1_ref[rows, :] = x1
        x1b_ref[rows, :] = x1.astype(BF16)


def _outproj(hm, hs, x, w_out, ln_g, ln_b, tm):
    m = x.shape[0]
    return pl.pallas_call(
        functools.partial(_outproj_kernel, tm=tm),
        grid=(m // tm,),
        in_specs=[
            pl.BlockSpec((tm, W_M), lambda i: (i, 0)),
            pl.BlockSpec((tm, W_SB), lambda i: (i, 0)),
            pl.BlockSpec((tm, D_MODEL), lambda i: (i, 0)),
            pl.BlockSpec((W_M + W_SB, D_MODEL), lambda i: (0, 0)),
            pl.BlockSpec((1, D_MODEL), lambda i: (0, 0)),
            pl.BlockSpec((1, D_MODEL), lambda i: (0, 0)),
        ],
        out_specs=[
            pl.BlockSpec((tm, D_MODEL), lambda i: (i, 0)),
            pl.BlockSpec((tm, D_MODEL), lambda i: (i, 0)),
        ],
        out_shape=[
            jax.ShapeDtypeStruct((m, D_MODEL), F32),
            jax.ShapeDtypeStruct((m, D_MODEL), BF16),
        ],
        compiler_params=_params(("parallel",), 48),
        name="outproj_ln",
    )(hm, hs, x, w_out, ln_g, ln_b)


UP_TN = 512
UP_NJ = D_FF_PAD // UP_TN
UP_CHUNK = 256
CARRY = 8


GELU_C1 = 0.7978845608028654
GELU_C3 = GELU_C1 * 0.044715


def _gelu_tanh(x):
    half = 0.5 * x
    return half + half * jnp.tanh(x * (GELU_C1 + GELU_C3 * (x * x)))


def _conv_taps(buf_ref, h, off, wc_ref, bc_ref):
    n = h.shape[0]
    buf_ref[CARRY + off:CARRY + off + n, :] = h
    h1 = buf_ref[CARRY + off - 1:CARRY + off - 1 + n, :]
    h2 = buf_ref[CARRY + off - 2:CARRY + off - 2 + n, :]
    return bc_ref[...] + wc_ref[0:1, :] * h2 + wc_ref[1:2, :] * h1 + wc_ref[2:3, :] * h


UP_SUB = UP_TN // LANES
UP_VALID = D_FF // LANES


def _up_weight_specs(tile_of):
    def spec(half, t):
        def index_map(*idx):
            return (0, half * UP_VALID + jnp.minimum(UP_SUB * tile_of(*idx) + t, UP_VALID - 1))
        return pl.BlockSpec((D_MODEL, LANES), index_map)
    return [spec(half, t) for half in range(2) for t in range(UP_SUB)]


def _cast_up_weights(j, w_refs, wu_s, wg_s):
    for t in range(UP_SUB):
        valid = UP_SUB * j + t < UP_VALID
        lanes = slice(t * LANES, (t + 1) * LANES)
        wu_s[:, lanes] = jnp.where(valid, w_refs[t][...], 0.0).astype(BF16)
        wg_s[:, lanes] = jnp.where(valid, w_refs[UP_SUB + t][...], 0.0).astype(BF16)


def _ffn_up_prompt_kernel(x_ref, *refs, tm):
    w_refs = refs[:2 * UP_SUB]
    (wcu_ref, wcg_ref, bcu_ref, bcg_ref, pu_ref, pg_ref,
     act_ref, cu_ref, cg_ref, bu_s, bg_s, wu_ref, wg_ref) = refs[2 * UP_SUB:]
    i = pl.program_id(1)

    @pl.when(i == 0)
    def _():
        _cast_up_weights(pl.program_id(0), w_refs, wu_ref, wg_ref)
        bu_s[CARRY - 2:CARRY, :] = pu_ref[...]
        bg_s[CARRY - 2:CARRY, :] = pg_ref[...]

    @pl.when(i > 0)
    def _():
        bu_s[0:CARRY, :] = bu_s[tm:tm + CARRY, :]
        bg_s[0:CARRY, :] = bg_s[tm:tm + CARRY, :]

    for c in range(tm // UP_CHUNK):
        off = c * UP_CHUNK
        x = x_ref[off:off + UP_CHUNK, :]
        hg = _dot(x, wg_ref[...])
        hu = _dot(x, wu_ref[...])
        gate = _gelu_tanh(_conv_taps(bg_s, hg, off, wcg_ref, bcg_ref))
        cu = _conv_taps(bu_s, hu, off, wcu_ref, bcu_ref)
        act_ref[off:off + UP_CHUNK, :] = (gate * cu).astype(act_ref.dtype)
    cu_ref[...] = hu[UP_CHUNK - 2:UP_CHUNK, :]
    cg_ref[...] = hg[UP_CHUNK - 2:UP_CHUNK, :]


def _ffn_up_prompt(x1b, w_up, wcu, wcg, bcu, bcg, pu, pg, tm):
    m = x1b.shape[0]
    col = lambda j, i: (0, j)
    w_scratch = pltpu.VMEM((D_MODEL, UP_TN), BF16)
    return pl.pallas_call(
        functools.partial(_ffn_up_prompt_kernel, tm=tm),
        grid=(UP_NJ, m // tm),
        in_specs=[pl.BlockSpec((tm, D_MODEL), lambda j, i: (i, 0))] + _up_weight_specs(lambda j, i: j) + [
            pl.BlockSpec((CONV_W, UP_TN), col),
            pl.BlockSpec((CONV_W, UP_TN), col),
            pl.BlockSpec((1, UP_TN), col),
            pl.BlockSpec((1, UP_TN), col),
            pl.BlockSpec((CONV_W - 1, UP_TN), col),
            pl.BlockSpec((CONV_W - 1, UP_TN), col),
        ],
        out_specs=[
            pl.BlockSpec((tm, UP_TN), lambda j, i: (i, j)),
            pl.BlockSpec((CONV_W - 1, UP_TN), col),
            pl.BlockSpec((CONV_W - 1, UP_TN), col),
        ],
        out_shape=[
            jax.ShapeDtypeStruct((m, D_FF_PAD), BF16),
            jax.ShapeDtypeStruct((CONV_W - 1, D_FF_PAD), F32),
            jax.ShapeDtypeStruct((CONV_W - 1, D_FF_PAD), F32),
        ],
        scratch_shapes=[pltpu.VMEM((tm + CARRY, UP_TN), F32), pltpu.VMEM((tm + CARRY, UP_TN), F32),
                        w_scratch, w_scratch],
        compiler_params=_params(("parallel", "arbitrary"), 56),
        name="ffn_up_prompt",
    )(x1b, *([w_up] * (2 * UP_SUB)), wcu, wcg, bcu, bcg, pu, pg)


def _conv_taps_seq(buf_ref, h, s0_ref, s1_ref, wc_ref, bc_ref, seg):
    n = h.shape[0]
    n_seq = n // seg
    h1s, h2s = [], []
    for s in range(h.shape[1] // LANES):
        lanes = slice(s * LANES, (s + 1) * LANES)
        buf_ref[s, CARRY:CARRY + n, :] = h[:, lanes]
        buf_ref[s, pl.ds(CARRY - 1, n_seq, stride=seg), :] = s1_ref[:, lanes]
        h1s.append(buf_ref[s, CARRY - 1:CARRY - 1 + n, :])
        buf_ref[s, pl.ds(CARRY - 2, n_seq, stride=seg), :] = s0_ref[:, lanes]
        h2s.append(buf_ref[s, CARRY - 2:CARRY - 2 + n, :])
    h1 = jnp.concatenate(h1s, axis=1)
    h2 = jnp.concatenate(h2s, axis=1)
    return bc_ref[...] + wc_ref[0:1, :] * h2 + wc_ref[1:2, :] * h1 + wc_ref[2:3, :] * h


def _ffn_up_sample_kernel(x_ref, *refs, seg):
    w_refs = refs[:2 * UP_SUB]
    (wcu_ref, wcg_ref, bcu_ref, bcg_ref, s0u_ref, s1u_ref, s0g_ref, s1g_ref,
     act_ref, hu_ref, hg_ref, bu_s, bg_s, wu_ref, wg_ref) = refs[2 * UP_SUB:]
    _cast_up_weights(pl.program_id(0), w_refs, wu_ref, wg_ref)
    x = x_ref[...]
    hu = _dot(x, wu_ref[...])
    hg = _dot(x, wg_ref[...])
    cu = _conv_taps_seq(bu_s, hu, s0u_ref, s1u_ref, wcu_ref, bcu_ref, seg)
    cg = _conv_taps_seq(bg_s, hg, s0g_ref, s1g_ref, wcg_ref, bcg_ref, seg)
    act_ref[...] = (_gelu_tanh(cg) * cu).astype(act_ref.dtype)
    hu_ref[...] = hu
    hg_ref[...] = hg


def _ffn_up_sample(x1b, w_up, wcu, wcg, bcu, bcg, s0u, s1u, s0g, s1g, seg):
    m = x1b.shape[0]
    n_seq = m // seg
    col = lambda j: (0, j)
    w_scratch = pltpu.VMEM((D_MODEL, UP_TN), BF16)
    c_spec = pl.BlockSpec((CONV_W, UP_TN), col)
    b_spec = pl.BlockSpec((1, UP_TN), col)
    s_spec = pl.BlockSpec((n_seq, UP_TN), col)
    o_spec = pl.BlockSpec((m, UP_TN), col)
    return pl.pallas_call(
        functools.partial(_ffn_up_sample_kernel, seg=seg),
        grid=(UP_NJ,),
        in_specs=[pl.BlockSpec((m, D_MODEL), lambda j: (0, 0))] + _up_weight_specs(lambda j: j) + [
            c_spec, c_spec, b_spec, b_spec, s_spec, s_spec, s_spec, s_spec],
        out_specs=[o_spec, o_spec, o_spec],
        out_shape=[
            jax.ShapeDtypeStruct((m, D_FF_PAD), BF16),
            jax.ShapeDtypeStruct((m, D_FF_PAD), F32),
            jax.ShapeDtypeStruct((m, D_FF_PAD), F32),
        ],
        scratch_shapes=[pltpu.VMEM((UP_TN // LANES, m + CARRY, LANES), F32)] * 2 + [w_scratch, w_scratch],
        compiler_params=_params(("parallel",), 36),
        name="ffn_up_sample",
    )(x1b, *([w_up] * (2 * UP_SUB)), wcu, wcg, bcu, bcg, s0u, s1u, s0g, s1g)


DOWN_TK = 1408
DOWN_NK = D_FF_PAD // DOWN_TK
DOWN_LAST = D_FF - (DOWN_NK - 1) * DOWN_TK


def _ffn_down_kernel(a_ref, w_ref, x1_ref, g_ref, b_ref, y_ref, acc_ref):
    k = pl.program_id(1)

    @pl.when(k == 0)
    def _():
        acc_ref[...] = _dot(a_ref[...], w_ref[...])

    @pl.when((k > 0) & (k < DOWN_NK - 1))
    def _():
        acc_ref[...] += _dot(a_ref[...], w_ref[...])

    @pl.when(k == DOWN_NK - 1)
    def _():
        ffn = acc_ref[...] + _dot(a_ref[:, :DOWN_LAST], w_ref[:DOWN_LAST, :])
        y_ref[...] = _layer_norm(DN_ALPHA * x1_ref[...] + ffn, g_ref[...], b_ref[...])


def _ffn_down(act, w_down, x1, ln_g, ln_b, tm):
    m = act.shape[0]
    return pl.pallas_call(
        _ffn_down_kernel,
        grid=(m // tm, DOWN_NK),
        in_specs=[
            pl.BlockSpec((tm, DOWN_TK), lambda i, k: (i, k)),
            pl.BlockSpec((DOWN_TK, D_MODEL), lambda i, k: (k, 0)),
            pl.BlockSpec((tm, D_MODEL), lambda i, k: (i, 0)),
            pl.BlockSpec((1, D_MODEL), lambda i, k: (0, 0)),
            pl.BlockSpec((1, D_MODEL), lambda i, k: (0, 0)),
        ],
        out_specs=pl.BlockSpec((tm, D_MODEL), lambda i, k: (i, 0)),
        out_shape=jax.ShapeDtypeStruct((m, D_MODEL), F32),
        scratch_shapes=[pltpu.VMEM((tm, D_MODEL), F32)],
        compiler_params=_params(("parallel", "arbitrary"), 48),
        name="ffn_down_ln",
    )(act, w_down, x1, ln_g, ln_b)


def _pad_cols(a, width):
    return jnp.pad(a, ((0, 0), (0, width - a.shape[1])))


def _halves(a, dtype=F32):
    return (_pad_cols(a[:, :D_FF], D_FF_PAD).astype(dtype), _pad_cols(a[:, D_FF:], D_FF_PAD).astype(dtype))


def _unsplit(u, g):
    return jnp.concatenate([u[..., :D_FF], g[..., :D_FF]], axis=-1)


def kernel(x_prompt, x_sample, cache_k, cache_v, state_C, state_n, state_m, state_conv, page_table,
           w_in, b_gates, b_sb, w_mlstm_norm, w_out, ln1_g, ln1_b, w_up, w_conv, b_conv, w_down,
           ln2_g, ln2_b):
    n_b, seq, _ = x_prompt.shape
    n_s, n_q, _ = x_sample.shape
    assert n_b == 1

    wi = w_in[0]
    wi_t = wi.T
    w_a = wi_t[:GATE_COL].astype(BF16)
    w_b = wi_t[GATE_COL + N_GATES:].astype(BF16)
    w_gate = jnp.pad(wi_t[GATE_COL:GATE_COL + N_GATES], ((0, LANES - N_GATES), (0, 0))).astype(BF16)
    bg = _pad_cols(b_gates[0][None, :], LANES)
    wn = w_mlstm_norm[0][None, :]
    wo = w_out[0].astype(BF16)
    wcu, wcg = _halves(w_conv[0])
    bcu, bcg = _halves(b_conv[0][None, :])
    wd = w_down[0].astype(BF16)
    g1, b1 = ln1_g[0][None, :], ln1_b[0][None, :]
    g2, b2 = ln2_g[0][None, :], ln2_b[0][None, :]
    bsb = b_sb[0]

    xp = x_prompt.reshape(seq, D_MODEL)
    p_p, k_p, v_p, kb_p, vb_p, gt_p = _inproj(xp, w_a, w_b, w_gate, 1024)
    hm_p, c_p, nn_p, mm_p = _mlstm(
        p_p, gt_p, bg, wn,
        jnp.zeros((1, H_M, DK_M, DV_M), F32), jnp.zeros((1, H_M, 1, DK_M), F32),
        jnp.zeros((1, H_M, 1, 1), F32), batch=1, L=256, out_dtype=BF16)
    hs_p = _sb_prompt(p_p, kb_p, vb_p, bsb)
    x1_p, x1b_p = _outproj(hm_p, hs_p, xp, wo, g1, b1, 512)
    conv0 = jnp.zeros((CONV_W - 1, D_FF_PAD), F32)
    act_p, cu_p, cg_p = _ffn_up_prompt(x1b_p, w_up[0], wcu, wcg, bcu, bcg, conv0, conv0, 2048)
    y_p = _ffn_down(act_p, wd, x1_p, g2, b2, 512)

    xs = x_sample.reshape(n_s * n_q, D_MODEL)
    p_s, k_s, v_s, _, _, gt_s = _inproj(xs, w_a, w_b, w_gate, n_s * n_q)
    hm_s, c_s, nn_s, mm_s = _mlstm(
        p_s, gt_s, bg, wn, state_C[0], state_n[0][:, :, None, :], state_m[0][:, :, None, None],
        batch=n_s, L=n_q, out_dtype=F32)
    hs_s = _sb_sample(p_s, k_s, v_s, cache_k.reshape(-1, DH_SB), cache_v.reshape(-1, DH_SB),
                      page_table, bsb, n_q=n_q)
    x1_s, x1b_s = _outproj(hm_s, hs_s, xs, wo, g1, b1, n_s * n_q)
    s0u, s0g = _halves(state_conv[0][:, 0, :])
    s1u, s1g = _halves(state_conv[0][:, 1, :])
    act_s, hu_s, hg_s = _ffn_up_sample(x1b_s, w_up[0], wcu, wcg, bcu, bcg, s0u, s1u, s0g, s1g, n_q)
    y_s = _ffn_down(act_s, wd, x1_s, g2, b2, n_s * n_q)

    conv_s = _unsplit(hu_s.reshape(n_s, n_q, D_FF_PAD)[:, n_q - 2:], hg_s.reshape(n_s, n_q, D_FF_PAD)[:, n_q - 2:])
    return (
        y_p.reshape(1, seq, D_MODEL),
        y_s.reshape(n_s, n_q, D_MODEL),
        k_p.reshape(1, 1, seq, H_SB, DH_SB),
        v_p.reshape(1, 1, seq, H_SB, DH_SB),
        c_p[None],
        nn_p.reshape(1, 1, H_M, DK_M),
        mm_p.reshape(1, 1, H_M),
        _unsplit(cu_p, cg_p)[None, None],
        k_s.reshape(1, n_s, n_q, H_SB, DH_SB),
        v_s.reshape(1, n_s, n_q, H_SB, DH_SB),
        c_s[None],
        nn_s.reshape(1, n_s, H_M, DK_M),
        mm_s.reshape(1, n_s, H_M),
        conv_s[None],
    )
```

```python
import functools

import jax
import jax.numpy as jnp
from jax import lax
from jax.experimental import pallas as pl
from jax.experimental.pallas import tpu as pltpu

F32 = jnp.float32
BF16 = jnp.bfloat16

D_MODEL = 2048
H_M = 4
DK_M = 128
DV_M = 256
W_M = H_M * DV_M
H_SB = 8
DH_SB = 128
W_SB = H_SB * DH_SB
D_FF = 5504
D_FF_PAD = 5632
CONV_W = 3
PAGE_SIZE = 128
LN_EPS = 1e-5
RMS_EPS = 1e-6
DN_ALPHA = 2.0 ** 0.25
GATE_COL = 3072
N_GATES = 2 * H_M
LANES = 128
MIB = 1024 * 1024
LOG2E = 1.4426950408889634


def _params(semantics, vmem_mib, flags=None):
    return pltpu.CompilerParams(dimension_semantics=semantics,
                                vmem_limit_bytes=int(vmem_mib * MIB), flags=flags)


def _dot(a, b):
    return jnp.dot(a, b, preferred_element_type=F32)


def _dot_nt(a, b):
    return lax.dot_general(a, b, (((1,), (1,)), ((), ())), preferred_element_type=F32)


def _dot_tn(a, b):
    return lax.dot_general(a, b, (((0,), (0,)), ((), ())), preferred_element_type=F32)


def _softplus(z):
    return jnp.maximum(z, 0.0) + jnp.log1p(jnp.exp(-jnp.abs(z)))


def _softplus2(z):
    return jnp.maximum(z, 0.0) + jnp.log2(1.0 + jnp.exp2(-jnp.abs(z)))


IN_TN = 512
IN_NJ_A = GATE_COL // IN_TN
IN_NJ_P = 4096 // IN_TN
IN_NJ = IN_NJ_P + 4


def _inproj_kernel(x_ref, wa_ref, wb_ref, wg_ref, p_ref, k_ref, v_ref, kb_ref, vb_ref, g_ref, xb_ref):
    j = pl.program_id(1)

    @pl.when(j == 0)
    def _():
        xb = x_ref[...].astype(BF16)
        xb_ref[...] = xb
        g_ref[...] = _dot_nt(xb, wg_ref[...])

    @pl.when(j < IN_NJ_A)
    def _():
        p_ref[...] = _dot_nt(xb_ref[...], wa_ref[...])

    @pl.when(j >= IN_NJ_A)
    def _():
        acc = _dot_nt(xb_ref[...], wb_ref[...])

        @pl.when(j < IN_NJ_P)
        def _():
            p_ref[...] = acc

        @pl.when((j >= IN_NJ_P) & (j < IN_NJ_P + 2))
        def _():
            k_ref[...] = acc
            kb_ref[...] = acc.astype(BF16)

        @pl.when(j >= IN_NJ_P + 2)
        def _():
            v_ref[...] = acc
            vb_ref[...] = acc.astype(BF16)


def _inproj(x, w_a, w_b, w_gate, tm):
    m = x.shape[0]
    k_blk = lambda i, j: (i, jnp.clip(j - IN_NJ_P, 0, 1))
    v_blk = lambda i, j: (i, jnp.clip(j - IN_NJ_P - 2, 0, 1))
    return pl.pallas_call(
        _inproj_kernel,
        grid=(m // tm, IN_NJ),
        in_specs=[
            pl.BlockSpec((tm, D_MODEL), lambda i, j: (i, 0)),
            pl.BlockSpec((IN_TN, D_MODEL), lambda i, j: (jnp.minimum(j, IN_NJ_A - 1), 0)),
            pl.BlockSpec((IN_TN, D_MODEL), lambda i, j: (jnp.maximum(j - IN_NJ_A, 0), 0)),
            pl.BlockSpec((LANES, D_MODEL), lambda i, j: (0, 0)),
        ],
        out_specs=[
            pl.BlockSpec((tm, IN_TN), lambda i, j: (i, jnp.minimum(j, IN_NJ_P - 1))),
            pl.BlockSpec((tm, IN_TN), k_blk),
            pl.BlockSpec((tm, IN_TN), v_blk),
            pl.BlockSpec((tm, IN_TN), k_blk),
            pl.BlockSpec((tm, IN_TN), v_blk),
            pl.BlockSpec((tm, LANES), lambda i, j: (i, 0)),
        ],
        out_shape=[
            jax.ShapeDtypeStruct((m, 4096), F32),
            jax.ShapeDtypeStruct((m, W_SB), F32),
            jax.ShapeDtypeStruct((m, W_SB), F32),
            jax.ShapeDtypeStruct((m, W_SB), BF16),
            jax.ShapeDtypeStruct((m, W_SB), BF16),
            jax.ShapeDtypeStruct((m, LANES), F32),
        ],
        scratch_shapes=[pltpu.VMEM((tm, D_MODEL), BF16)],
        compiler_params=_params(("parallel", "arbitrary"), 54),
        name="inproj",
    )(x, w_a, w_b, w_gate)


def _mlstm_kernel(q_ref, k_ref, v_ref, o_ref, g_ref, bg_ref, wn_ref, c0_ref, n0_ref, m0_ref,
                  h_ref, cout_ref, nout_ref, mout_ref, c_s, n_s, m_s, *, L):
    c = pl.program_id(1)

    @pl.when(c == 0)
    def _():
        c_s[...] = c0_ref[0]
        n_s[...] = n0_ref[0]
        m_s[...] = m0_ref[0]

    g = g_ref[...] + bg_ref[...]
    row = lax.broadcasted_iota(jnp.int32, (L, L), 0)
    col = lax.broadcasted_iota(jnp.int32, (L, L), 1)
    causal = col <= row
    eye = col == row
    heads = range(H_M)
    dk = [slice(h * DK_M, (h + 1) * DK_M) for h in heads]
    dv = [slice(h * DV_M, (h + 1) * DV_M) for h in heads]
    qf = [q_ref[:, dk[h]] * (DK_M ** -0.5) for h in heads]
    kf = [k_ref[:, dk[h]] for h in heads]
    qb = [x.astype(BF16) for x in qf]
    kb = [x.astype(BF16) for x in kf]
    vb = [v_ref[:, dv[h]].astype(BF16) for h in heads]
    c_prev = [c_s[h] for h in heads]
    n_prev = [n_s[h] for h in heads]
    m_prev = [m_s[h] for h in heads]
    s_raw = [_dot_nt(qb[h], kb[h]) for h in heads]
    q_c = [_dot(qb[h], c_prev[h].astype(BF16)) for h in heads]

    w_intra, w_inter, w_end, decay, m_t, m_new = [], [], [], [], [], []
    for h in heads:
        li_col = g[:, h:h + 1]
        fg_col = g[:, H_M + h:H_M + h + 1]
        lf_col = -_softplus(-fg_col)
        li_row = jnp.sum(jnp.where(eye, li_col, 0.0), axis=0, keepdims=True)
        lf_row = jnp.sum(jnp.where(eye, lf_col, 0.0), axis=0, keepdims=True)
        b_row = jnp.sum(jnp.where(row <= col, lf_col, 0.0), axis=0, keepdims=True)
        b_col = jnp.sum(jnp.where(causal, lf_row, 0.0), axis=1, keepdims=True)
        b_last = b_col[L - 1:L, :]
        d = jnp.where(causal, b_col - b_row + li_row, -jnp.inf)
        a_col = b_col + m_prev[h]
        mt = jnp.maximum(a_col, jnp.max(d, axis=1, keepdims=True))
        mn = mt[L - 1:L, :]
        m_t.append(mt)
        m_new.append(mn)
        w_intra.append(jnp.exp(d - mt))
        w_inter.append(jnp.exp(a_col - mt))
        w_end.append(jnp.exp(b_last - b_col + li_col - mn))
        decay.append(jnp.exp(b_last + m_prev[h] - mn))

    kw = [kf[h] * w_end[h] for h in heads]
    k_v = [_dot_tn(kw[h].astype(BF16), vb[h]) for h in heads]
    s = [s_raw[h] * w_intra[h] for h in heads]
    s_v = [_dot(s[h].astype(BF16), vb[h]) for h in heads]
    for h in heads:
        c_s[h] = decay[h] * c_prev[h] + k_v[h]
        n_s[h] = decay[h] * n_prev[h] + jnp.sum(kw[h], axis=0, keepdims=True)
        m_s[h] = m_new[h]
    for h in heads:
        num = w_inter[h] * q_c[h] + s_v[h]
        den = (w_inter[h] * jnp.sum(qf[h] * n_prev[h], axis=1, keepdims=True)
               + jnp.sum(s[h], axis=1, keepdims=True))
        hh = num / jnp.maximum(jnp.abs(den), jnp.exp(-m_t[h]))
        hn = hh * lax.rsqrt(jnp.mean(hh * hh, axis=1, keepdims=True) + RMS_EPS)
        hn = hn * wn_ref[:, dv[h]]
        og = jax.nn.sigmoid(o_ref[:, dv[h]])
        h_ref[:, dv[h]] = (og * hn).astype(h_ref.dtype)

    @pl.when(c == pl.num_programs(1) - 1)
    def _():
        cout_ref[0] = c_s[...]
        nout_ref[0] = n_s[...]
        mout_ref[0] = m_s[...]


def _mlstm(p, gates, bg, wn, c0, n0, m0, *, batch, L, out_dtype):
    t_total = p.shape[0]
    nc = t_total // (batch * L)
    row = lambda b, c: b * nc + c
    return pl.pallas_call(
        functools.partial(_mlstm_kernel, L=L),
        grid=(batch, nc),
        in_specs=[
            pl.BlockSpec((L, H_M * DK_M), lambda b, c: (row(b, c), 0)),
            pl.BlockSpec((L, H_M * DK_M), lambda b, c: (row(b, c), 1)),
            pl.BlockSpec((L, W_M), lambda b, c: (row(b, c), 1)),
            pl.BlockSpec((L, W_M), lambda b, c: (row(b, c), 2)),
            pl.BlockSpec((L, LANES), lambda b, c: (row(b, c), 0)),
            pl.BlockSpec((1, LANES), lambda b, c: (0, 0)),
            pl.BlockSpec((1, W_M), lambda b, c: (0, 0)),
            pl.BlockSpec((1, H_M, DK_M, DV_M), lambda b, c: (b, 0, 0, 0)),
            pl.BlockSpec((1, H_M, 1, DK_M), lambda b, c: (b, 0, 0, 0)),
            pl.BlockSpec((1, H_M, 1, 1), lambda b, c: (b, 0, 0, 0)),
        ],
        out_specs=[
            pl.BlockSpec((L, W_M), lambda b, c: (row(b, c), 0)),
            pl.BlockSpec((1, H_M, DK_M, DV_M), lambda b, c: (b, 0, 0, 0)),
            pl.BlockSpec((1, H_M, 1, DK_M), lambda b, c: (b, 0, 0, 0)),
            pl.BlockSpec((1, H_M, 1, 1), lambda b, c: (b, 0, 0, 0)),
        ],
        out_shape=[
            jax.ShapeDtypeStruct((t_total, W_M), out_dtype),
            jax.ShapeDtypeStruct((batch, H_M, DK_M, DV_M), F32),
            jax.ShapeDtypeStruct((batch, H_M, 1, DK_M), F32),
            jax.ShapeDtypeStruct((batch, H_M, 1, 1), F32),
        ],
        scratch_shapes=[
            pltpu.VMEM((H_M, DK_M, DV_M), F32),
            pltpu.VMEM((H_M, 1, DK_M), F32),
            pltpu.VMEM((H_M, 1, 1), F32),
        ],
        compiler_params=_params(("parallel", "arbitrary"), 32),
        name="mlstm",
    )(p, p, p, p, gates, bg, wn, c0, n0, m0)


SBP_BQ = 512
SBP_BK = 256
SBP_NH = 4
SB_QSCALE = (DH_SB ** -0.5) * LOG2E


def _sb_prompt_kernel(bsb_ref, q_ref, k_ref, v_ref, o_ref, acc_ref, rest_ref,
                      u0_ref, u1_ref, t0_ref, t1_ref):
    hp = pl.program_id(0)
    i = pl.program_id(1)
    bq, bk = SBP_BQ, SBP_BK
    per_q = bq // bk
    nk = per_q * (i + 1)
    heads = [slice(hh * DH_SB, (hh + 1) * DH_SB) for hh in range(SBP_NH)]
    qs = [(q_ref[:, hs] * SB_QSCALE).astype(BF16) for hs in heads]
    bias = [bsb_ref[hp * SBP_NH + hh] * LOG2E for hh in range(SBP_NH)]
    later_aug = _later_aug(bk)
    row = lax.broadcasted_iota(jnp.int32, (bq, bk), 0)
    col = lax.broadcasted_iota(jnp.int32, (bq, bk), 1)
    acc_ref[...] = jnp.zeros_like(acc_ref)
    rest_ref[...] = jnp.zeros_like(rest_ref)

    def key_rows(jj):
        return pl.ds(pl.multiple_of((nk - 1 - jj) * bk, bk), bk)

    def scores(jj, u_ref, t_ref, diag):
        rows = key_rows(jj)
        mask = None if diag is None else (col + bk * (per_q - 1 - diag)) < row
        zs = [_dot_nt(qs[hh], k_ref[rows, hs]) + bias[hh] for hh, hs in enumerate(heads)]
        sps = [_softplus2(z) for z in zs]
        spms = sps if mask is None else [jnp.where(mask, sp, 0.0) for sp in sps]
        eas = [_dot(spm.astype(BF16), later_aug) for spm in spms]
        for hh in range(SBP_NH):
            u = zs[hh] - sps[hh] - eas[hh][:, :bk]
            u_ref[hh] = u if mask is None else jnp.where(mask, u, -jnp.inf)
            t_ref[hh] = eas[hh][:, bk:]

    def values(jj, u_ref, t_ref):
        rows = key_rows(jj)
        rests = [rest_ref[hh] for hh in range(SBP_NH)]
        a_s = [jnp.exp2(u_ref[hh] + jnp.concatenate([rests[hh]] * (bk // LANES), axis=1))
               for hh in range(SBP_NH)]
        outs = [_dot(a_s[hh].astype(BF16), v_ref[rows, hs]) for hh, hs in enumerate(heads)]
        for hh in range(SBP_NH):
            acc_ref[hh] += outs[hh]
            rest_ref[hh] = rests[hh] - t_ref[hh]

    scores(0, u0_ref, t0_ref, 0)
    scores(1, u1_ref, t1_ref, 1)
    values(0, u0_ref, t0_ref)

    def fused(jj, ua_ref, ta_ref, ub_ref, tb_ref):
        rows_a, rows_b = key_rows(jj), key_rows(jj - 1)
        nh = range(SBP_NH)
        zs = [_dot_nt(qs[hh], k_ref[rows_a, hs]) + bias[hh] for hh, hs in enumerate(heads)]
        rests = [rest_ref[hh] for hh in nh]
        a_s = [jnp.exp2(ub_ref[hh] + jnp.concatenate([rests[hh]] * (bk // LANES), axis=1)) for hh in nh]
        outs = [_dot(a_s[hh].astype(BF16), v_ref[rows_b, hs]) for hh, hs in enumerate(heads)]
        sps = [_softplus2(z) for z in zs]
        eas = [_dot(sp.astype(BF16), later_aug) for sp in sps]
        for hh in nh:
            acc_ref[hh] += outs[hh]
            rest_ref[hh] = rests[hh] - tb_ref[hh]
        for hh in nh:
            ua_ref[hh] = zs[hh] - sps[hh] - eas[hh][:, :bk]
            ta_ref[hh] = eas[hh][:, bk:]

    def pair(p, carry):
        jj = per_q * (p + 1)
        fused(jj, u0_ref, t0_ref, u1_ref, t1_ref)
        fused(jj + 1, u1_ref, t1_ref, u0_ref, t0_ref)
        return carry

    lax.fori_loop(0, i, pair, 0)
    values(nk - 1, u1_ref, t1_ref)
    for hh, hs in enumerate(heads):
        o_ref[:, hs] = acc_ref[hh].astype(o_ref.dtype)


def _sb_prompt(p, kb, vb, b_sb):
    assert SBP_BQ == 2 * SBP_BK
    T = p.shape[0]
    width = SBP_NH * DH_SB
    q_col0 = 3072 // width
    u_shape = pltpu.VMEM((SBP_NH, SBP_BQ, SBP_BK), F32)
    t_shape = pltpu.VMEM((SBP_NH, SBP_BQ, LANES), F32)
    return pl.pallas_call(
        _sb_prompt_kernel,
        grid=(H_SB // SBP_NH, T // SBP_BQ),
        in_specs=[
            pl.BlockSpec(memory_space=pltpu.SMEM),
            pl.BlockSpec((SBP_BQ, width), lambda h, i: (i, q_col0 + h)),
            pl.BlockSpec((T, width), lambda h, i: (0, h)),
            pl.BlockSpec((T, width), lambda h, i: (0, h)),
        ],
        out_specs=pl.BlockSpec((SBP_BQ, width), lambda h, i: (i, h)),
        out_shape=jax.ShapeDtypeStruct((T, W_SB), BF16),
        scratch_shapes=[pltpu.VMEM((SBP_NH, SBP_BQ, DH_SB), F32), t_shape,
                        u_shape, u_shape, t_shape, t_shape],
        compiler_params=_params(("parallel", "arbitrary"), 52),
        name="sb_prompt",
    )(b_sb, p, kb, vb)


SBS_PPS = 16


def _later_aug(n):
    row = lax.broadcasted_iota(jnp.int32, (n, n + LANES), 0)
    col = lax.broadcasted_iota(jnp.int32, (n, n + LANES), 1)
    return ((row > col) | (col >= n)).astype(BF16)


def _sb_sample_kernel(pt_ref, bsb_ref, q_ref, kn_ref, vn_ref, *rest, n_q):
    page_refs = rest[:2 * SBS_PPS]
    o_ref, acc_ref, rest_ref = rest[2 * SBS_PPS:]
    g = pl.program_id(1)
    rows = H_SB * n_q

    qs = [(q_ref[:, h * DH_SB:(h + 1) * DH_SB] * SB_QSCALE).astype(BF16) for h in range(H_SB)]
    bias = jnp.concatenate([jnp.full((n_q, PAGE_SIZE), bsb_ref[h] * LOG2E, F32) for h in range(H_SB)],
                           axis=0)
    later_aug = _later_aug(PAGE_SIZE)

    def head_dots_nt(ks):
        return jnp.concatenate([_dot_nt(qs[h], ks[h]) for h in range(H_SB)], axis=0)

    def head_dots(ab, vs):
        return jnp.concatenate([_dot(ab[h * n_q:(h + 1) * n_q, :], vs[h]) for h in range(H_SB)], axis=0)

    @pl.when(g == 0)
    def _():
        pad = jnp.zeros((PAGE_SIZE - n_q, DH_SB), BF16)
        ks = [jnp.concatenate([kn_ref[:, h * DH_SB:(h + 1) * DH_SB].astype(BF16), pad], axis=0)
              for h in range(H_SB)]
        vs = [jnp.concatenate([vn_ref[:, h * DH_SB:(h + 1) * DH_SB].astype(BF16), pad], axis=0)
              for h in range(H_SB)]
        r = lax.broadcasted_iota(jnp.int32, (rows, PAGE_SIZE), 0)
        cidx = lax.broadcasted_iota(jnp.int32, (rows, PAGE_SIZE), 1)
        mask = cidx < (r % n_q)
        z = head_dots_nt(ks) + bias
        sp = _softplus2(z)
        ea = _dot(jnp.where(mask, sp, 0.0).astype(BF16), later_aug)
        a = jnp.where(mask, jnp.exp2(z - sp - ea[:, :PAGE_SIZE]), 0.0)
        acc_ref[...] = head_dots(a.astype(BF16), vs)
        rest_ref[...] = -ea[:, PAGE_SIZE:]

    def head_rows(refs, h):
        return jnp.concatenate([r[pl.ds(h, PAGE_SIZE, stride=H_SB), :] for r in refs],
                               axis=0).astype(BF16)

    ks = [head_rows(page_refs[:SBS_PPS], h) for h in range(H_SB)]
    vs = [head_rows(page_refs[SBS_PPS:], h) for h in range(H_SB)]
    z = head_dots_nt(ks) + jnp.concatenate([bias] * SBS_PPS, axis=1)
    sp = _softplus2(z)
    spb = sp.astype(BF16)
    rest_v = rest_ref[...]
    es, rests = [], []
    for c in range(SBS_PPS):
        ea = _dot(spb[:, c * PAGE_SIZE:(c + 1) * PAGE_SIZE], later_aug)
        es.append(ea[:, :PAGE_SIZE])
        rests.append(rest_v)
        rest_v = rest_v - ea[:, PAGE_SIZE:]
    a = jnp.exp2(z - sp - jnp.concatenate(es, axis=1) + jnp.concatenate(rests, axis=1))
    acc = acc_ref[...] + head_dots(a.astype(BF16), vs)
    acc_ref[...] = acc
    rest_ref[...] = rest_v

    @pl.when(g == pl.num_programs(1) - 1)
    def _():
        for h in range(H_SB):
            o_ref[:, h * DH_SB:(h + 1) * DH_SB] = acc[h * n_q:(h + 1) * n_q, :]


def _sb_sample(p, k_new, v_new, cache_k, cache_v, page_table, b_sb, *, n_q):
    n_seq, n_pages = page_table.shape
    n_groups = n_pages // SBS_PPS

    def page_spec(c):
        return pl.BlockSpec(
            (PAGE_SIZE * H_SB, DH_SB),
            lambda b, g, pt: (pt[b, n_pages - 1 - (g * SBS_PPS + c)], 0))

    return pl.pallas_call(
        functools.partial(_sb_sample_kernel, n_q=n_q),
        grid_spec=pltpu.PrefetchScalarGridSpec(
            num_scalar_prefetch=1,
            grid=(n_seq, n_groups),
            in_specs=[
                pl.BlockSpec(memory_space=pltpu.SMEM),
                pl.BlockSpec((n_q, W_SB), lambda b, g, pt: (b, 3)),
                pl.BlockSpec((n_q, W_SB), lambda b, g, pt: (b, 0)),
                pl.BlockSpec((n_q, W_SB), lambda b, g, pt: (b, 0)),
            ] + [page_spec(c) for c in range(SBS_PPS)] * 2,
            out_specs=pl.BlockSpec((n_q, W_SB), lambda b, g, pt: (b, 0)),
            scratch_shapes=[pltpu.VMEM((H_SB * n_q, DH_SB), F32),
                            pltpu.VMEM((H_SB * n_q, LANES), F32)],
        ),
        out_shape=jax.ShapeDtypeStruct((n_seq * n_q, W_SB), F32),
        compiler_params=_params(("parallel", "arbitrary"), 48),
        name="sb_sample",
    )(page_table, b_sb, p, k_new, v_new,
      *([cache_k] * SBS_PPS), *([cache_v] * SBS_PPS))


ROW_CHUNK = 256


def _layer_norm(x, g, b):
    mu = jnp.mean(x, axis=-1, keepdims=True)
    xc = x - mu
    var = jnp.mean(xc * xc, axis=-1, keepdims=True)
    return xc * lax.rsqrt(var + LN_EPS) * g + b


def _outproj_kernel(hm_ref, hs_ref, x_ref, w_ref, g_ref, b_ref, x1_ref, x1b_ref, *, tm):
    for c in range(tm // ROW_CHUNK):
        rows = slice(c * ROW_CHUNK, (c + 1) * ROW_CHUNK)
        h = jnp.concatenate([hm_ref[rows, :].astype(BF16), hs_ref[rows, :].astype(BF16)], axis=1)
        mix = _dot(h, w_ref[...])
        x1 = _layer_norm(DN_ALPHA * x_ref[rows, :] + mix, g_ref[...], b_ref[...])
        x1_ref[rows, :] = x1
        x1b_ref[rows, :] = x1.astype(BF16)


def _outproj(hm, hs, x, w_out, ln_g, ln_b, tm):
    m = x.shape[0]
    return pl.pallas_call(
        functools.partial(_outproj_kernel, tm=tm),
        grid=(m // tm,),
        in_specs=[
            pl.BlockSpec((tm, W_M), lambda i: (i, 0)),
            pl.BlockSpec((tm, W_SB), lambda i: (i, 0)),
            pl.BlockSpec((tm, D_MODEL), lambda i: (i, 0)),
            pl.BlockSpec((W_M + W_SB, D_MODEL), lambda i: (0, 0)),
            pl.BlockSpec((1, D_MODEL), lambda i: (0, 0)),
            pl.BlockSpec((1, D_MODEL), lambda i: (0, 0)),
        ],
        out_specs=[
            pl.BlockSpec((tm, D_MODEL), lambda i: (i, 0)),
            pl.BlockSpec((tm, D_MODEL), lambda i: (i, 0)),
        ],
        out_shape=[
            jax.ShapeDtypeStruct((m, D_MODEL), F32),
            jax.ShapeDtypeStruct((m, D_MODEL), BF16),
        ],
        compiler_params=_params(("parallel",), 48),
        name="outproj_ln",
    )(hm, hs, x, w_out, ln_g, ln_b)


UP_TN = 512
UP_NJ = D_FF_PAD // UP_TN
UP_CHUNK = 256
CARRY = 8


GELU_C1 = 0.7978845608028654
GELU_C3 = GELU_C1 * 0.044715


def _gelu_tanh(x):
    half = 0.5 * x
    return half + half * jnp.tanh(x * (GELU_C1 + GELU_C3 * (x * x)))


def _conv_taps(buf_ref, h, off, wc_ref, bc_ref):
    n = h.shape[0]
    buf_ref[CARRY + off:CARRY + off + n, :] = h
    h1 = buf_ref[CARRY + off - 1:CARRY + off - 1 + n, :]
    h2 = buf_ref[CARRY + off - 2:CARRY + off - 2 + n, :]
    return bc_ref[...] + wc_ref[0:1, :] * h2 + wc_ref[1:2, :] * h1 + wc_ref[2:3, :] * h


UP_SUB = UP_TN // LANES
UP_VALID = D_FF // LANES


def _up_weight_specs(tile_of):
    def spec(half, t):
        def index_map(*idx):
            return (0, half * UP_VALID + jnp.minimum(UP_SUB * tile_of(*idx) + t, UP_VALID - 1))
        return pl.BlockSpec((D_MODEL, LANES), index_map)
    return [spec(half, t) for half in range(2) for t in range(UP_SUB)]


def _cast_up_weights(j, w_refs, wu_s, wg_s):
    for t in range(UP_SUB):
        valid = UP_SUB * j + t < UP_VALID
        lanes = slice(t * LANES, (t + 1) * LANES)
        wu_s[:, lanes] = jnp.where(valid, w_refs[t][...], 0.0).astype(BF16)
        wg_s[:, lanes] = jnp.where(valid, w_refs[UP_SUB + t][...], 0.0).astype(BF16)


def _ffn_up_prompt_kernel(x_ref, *refs, tm):
    w_refs = refs[:2 * UP_SUB]
    (wcu_ref, wcg_ref, bcu_ref, bcg_ref, pu_ref, pg_ref,
     act_ref, cu_ref, cg_ref, bu_s, bg_s, wu_ref, wg_ref) = refs[2 * UP_SUB:]
    i = pl.program_id(1)

    @pl.when(i == 0)
    def _():
        _cast_up_weights(pl.program_id(0), w_refs, wu_ref, wg_ref)
        bu_s[CARRY - 2:CARRY, :] = pu_ref[...]
        bg_s[CARRY - 2:CARRY, :] = pg_ref[...]

    @pl.when(i > 0)
    def _():
        bu_s[0:CARRY, :] = bu_s[tm:tm + CARRY, :]
        bg_s[0:CARRY, :] = bg_s[tm:tm + CARRY, :]

    for c in range(tm // UP_CHUNK):
        off = c * UP_CHUNK
        x = x_ref[off:off + UP_CHUNK, :]
        hg = _dot(x, wg_ref[...])
        hu = _dot(x, wu_ref[...])
        gate = _gelu_tanh(_conv_taps(bg_s, hg, off, wcg_ref, bcg_ref))
        cu = _conv_taps(bu_s, hu, off, wcu_ref, bcu_ref)
        act_ref[off:off + UP_CHUNK, :] = (gate * cu).astype(act_ref.dtype)
    cu_ref[...] = hu[UP_CHUNK - 2:UP_CHUNK, :]
    cg_ref[...] = hg[UP_CHUNK - 2:UP_CHUNK, :]


def _ffn_up_prompt(x1b, w_up, wcu, wcg, bcu, bcg, pu, pg, tm):
    m = x1b.shape[0]
    col = lambda j, i: (0, j)
    w_scratch = pltpu.VMEM((D_MODEL, UP_TN), BF16)
    return pl.pallas_call(
        functools.partial(_ffn_up_prompt_kernel, tm=tm),
        grid=(UP_NJ, m // tm),
        in_specs=[pl.BlockSpec((tm, D_MODEL), lambda j, i: (i, 0))] + _up_weight_specs(lambda j, i: j) + [
            pl.BlockSpec((CONV_W, UP_TN), col),
            pl.BlockSpec((CONV_W, UP_TN), col),
            pl.BlockSpec((1, UP_TN), col),
            pl.BlockSpec((1, UP_TN), col),
            pl.BlockSpec((CONV_W - 1, UP_TN), col),
            pl.BlockSpec((CONV_W - 1, UP_TN), col),
        ],
        out_specs=[
            pl.BlockSpec((tm, UP_TN), lambda j, i: (i, j)),
            pl.BlockSpec((CONV_W - 1, UP_TN), col),
            pl.BlockSpec((CONV_W - 1, UP_TN), col),
        ],
        out_shape=[
            jax.ShapeDtypeStruct((m, D_FF_PAD), BF16),
            jax.ShapeDtypeStruct((CONV_W - 1, D_FF_PAD), F32),
            jax.ShapeDtypeStruct((CONV_W - 1, D_FF_PAD), F32),
        ],
        scratch_shapes=[pltpu.VMEM((tm + CARRY, UP_TN), F32), pltpu.VMEM((tm + CARRY, UP_TN), F32),
                        w_scratch, w_scratch],
        compiler_params=_params(("parallel", "arbitrary"), 56),
        name="ffn_up_prompt",
    )(x1b, *([w_up] * (2 * UP_SUB)), wcu, wcg, bcu, bcg, pu, pg)


def _conv_taps_seq(buf_ref, h, s0_ref, s1_ref, wc_ref, bc_ref, seg):
    n = h.shape[0]
    n_seq = n // seg
    h1s, h2s = [], []
    for s in range(h.shape[1] // LANES):
        lanes = slice(s * LANES, (s + 1) * LANES)
        buf_ref[s, CARRY:CARRY + n, :] = h[:, lanes]
        buf_ref[s, pl.ds(CARRY - 1, n_seq, stride=seg), :] = s1_ref[:, lanes]
        h1s.append(buf_ref[s, CARRY - 1:CARRY - 1 + n, :])
        buf_ref[s, pl.ds(CARRY - 2, n_seq, stride=seg), :] = s0_ref[:, lanes]
        h2s.append(buf_ref[s, CARRY - 2:CARRY - 2 + n, :])
    h1 = jnp.concatenate(h1s, axis=1)
    h2 = jnp.concatenate(h2s, axis=1)
    return bc_ref[...] + wc_ref[0:1, :] * h2 + wc_ref[1:2, :] * h1 + wc_ref[2:3, :] * h


def _ffn_up_sample_kernel(x_ref, *refs, seg):
    w_refs = refs[:2 * UP_SUB]
    (wcu_ref, wcg_ref, bcu_ref, bcg_ref, s0u_ref, s1u_ref, s0g_ref, s1g_ref,
     act_ref, hu_ref, hg_ref, bu_s, bg_s, wu_ref, wg_ref) = refs[2 * UP_SUB:]
    _cast_up_weights(pl.program_id(0), w_refs, wu_ref, wg_ref)
    x = x_ref[...]
    hu = _dot(x, wu_ref[...])
    hg = _dot(x, wg_ref[...])
    cu = _conv_taps_seq(bu_s, hu, s0u_ref, s1u_ref, wcu_ref, bcu_ref, seg)
    cg = _conv_taps_seq(bg_s, hg, s0g_ref, s1g_ref, wcg_ref, bcg_ref, seg)
    act_ref[...] = (_gelu_tanh(cg) * cu).astype(act_ref.dtype)
    hu_ref[...] = hu
    hg_ref[...] = hg


def _ffn_up_sample(x1b, w_up, wcu, wcg, bcu, bcg, s0u, s1u, s0g, s1g, seg):
    m = x1b.shape[0]
    n_seq = m // seg
    col = lambda j: (0, j)
    w_scratch = pltpu.VMEM((D_MODEL, UP_TN), BF16)
    c_spec = pl.BlockSpec((CONV_W, UP_TN), col)
    b_spec = pl.BlockSpec((1, UP_TN), col)
    s_spec = pl.BlockSpec((n_seq, UP_TN), col)
    o_spec = pl.BlockSpec((m, UP_TN), col)
    return pl.pallas_call(
        functools.partial(_ffn_up_sample_kernel, seg=seg),
        grid=(UP_NJ,),
        in_specs=[pl.BlockSpec((m, D_MODEL), lambda j: (0, 0))] + _up_weight_specs(lambda j: j) + [
            c_spec, c_spec, b_spec, b_spec, s_spec, s_spec, s_spec, s_spec],
        out_specs=[o_spec, o_spec, o_spec],
        out_shape=[
            jax.ShapeDtypeStruct((m, D_FF_PAD), BF16),
            jax.ShapeDtypeStruct((m, D_FF_PAD), F32),
            jax.ShapeDtypeStruct((m, D_FF_PAD), F32),
        ],
        scratch_shapes=[pltpu.VMEM((UP_TN // LANES, m + CARRY, LANES), F32)] * 2 + [w_scratch, w_scratch],
        compiler_params=_params(("parallel",), 36),
        name="ffn_up_sample",
    )(x1b, *([w_up] * (2 * UP_SUB)), wcu, wcg, bcu, bcg, s0u, s1u, s0g, s1g)


DOWN_TK = 2816
DOWN_NK = D_FF_PAD // DOWN_TK
DOWN_LAST = D_FF - (DOWN_NK - 1) * DOWN_TK


def _ffn_down_kernel(a_ref, w_ref, x1_ref, g_ref, b_ref, y_ref, acc_ref):
    k = pl.program_id(1)

    @pl.when(k == 0)
    def _():
        acc_ref[...] = _dot(a_ref[...], w_ref[...])

    @pl.when((k > 0) & (k < DOWN_NK - 1))
    def _():
        acc_ref[...] += _dot(a_ref[...], w_ref[...])

    @pl.when(k == DOWN_NK - 1)
    def _():
        ffn = acc_ref[...] + _dot(a_ref[:, :DOWN_LAST], w_ref[:DOWN_LAST, :])
        y_ref[...] = _layer_norm(DN_ALPHA * x1_ref[...] + ffn, g_ref[...], b_ref[...])


def _ffn_down(act, w_down, x1, ln_g, ln_b, tm):
    m = act.shape[0]
    return pl.pallas_call(
        _ffn_down_kernel,
        grid=(m // tm, DOWN_NK),
        in_specs=[
            pl.BlockSpec((tm, DOWN_TK), lambda i, k: (i, k)),
            pl.BlockSpec((DOWN_TK, D_MODEL), lambda i, k: (k, 0)),
            pl.BlockSpec((tm, D_MODEL), lambda i, k: (i, 0)),
            pl.BlockSpec((1, D_MODEL), lambda i, k: (0, 0)),
            pl.BlockSpec((1, D_MODEL), lambda i, k: (0, 0)),
        ],
        out_specs=pl.BlockSpec((tm, D_MODEL), lambda i, k: (i, 0)),
        out_shape=jax.ShapeDtypeStruct((m, D_MODEL), F32),
        scratch_shapes=[pltpu.VMEM((tm, D_MODEL), F32)],
        compiler_params=_params(("parallel", "arbitrary"), 56),
        name="ffn_down_ln",
    )(act, w_down, x1, ln_g, ln_b)


def _pad_cols(a, width):
    return jnp.pad(a, ((0, 0), (0, width - a.shape[1])))


def _halves(a, dtype=F32):
    return (_pad_cols(a[:, :D_FF], D_FF_PAD).astype(dtype), _pad_cols(a[:, D_FF:], D_FF_PAD).astype(dtype))


def _unsplit(u, g):
    return jnp.concatenate([u[..., :D_FF], g[..., :D_FF]], axis=-1)


def kernel(x_prompt, x_sample, cache_k, cache_v, state_C, state_n, state_m, state_conv, page_table,
           w_in, b_gates, b_sb, w_mlstm_norm, w_out, ln1_g, ln1_b, w_up, w_conv, b_conv, w_down,
           ln2_g, ln2_b):
    n_b, seq, _ = x_prompt.shape
    n_s, n_q, _ = x_sample.shape
    assert n_b == 1

    wi = w_in[0]
    wi_t = wi.T
    w_a = wi_t[:GATE_COL].astype(BF16)
    w_b = wi_t[GATE_COL + N_GATES:].astype(BF16)
    w_gate = jnp.pad(wi_t[GATE_COL:GATE_COL + N_GATES], ((0, LANES - N_GATES), (0, 0))).astype(BF16)
    bg = _pad_cols(b_gates[0][None, :], LANES)
    wn = w_mlstm_norm[0][None, :]
    wo = w_out[0].astype(BF16)
    wcu, wcg = _halves(w_conv[0])
    bcu, bcg = _halves(b_conv[0][None, :])
    wd = w_down[0].astype(BF16)
    g1, b1 = ln1_g[0][None, :], ln1_b[0][None, :]
    g2, b2 = ln2_g[0][None, :], ln2_b[0][None, :]
    bsb = b_sb[0]

    xp = x_prompt.reshape(seq, D_MODEL)
    p_p, k_p, v_p, kb_p, vb_p, gt_p = _inproj(xp, w_a, w_b, w_gate, 1024)
    hm_p, c_p, nn_p, mm_p = _mlstm(
        p_p, gt_p, bg, wn,
        jnp.zeros((1, H_M, DK_M, DV_M), F32), jnp.zeros((1, H_M, 1, DK_M), F32),
        jnp.zeros((1, H_M, 1, 1), F32), batch=1, L=256, out_dtype=BF16)
    hs_p = _sb_prompt(p_p, kb_p, vb_p, bsb)
    x1_p, x1b_p = _outproj(hm_p, hs_p, xp, wo, g1, b1, 512)
    conv0 = jnp.zeros((CONV_W - 1, D_FF_PAD), F32)
    act_p, cu_p, cg_p = _ffn_up_prompt(x1b_p, w_up[0], wcu, wcg, bcu, bcg, conv0, conv0, 2048)
    y_p = _ffn_down(act_p, wd, x1_p, g2, b2, 512)

    xs = x_sample.reshape(n_s * n_q, D_MODEL)
    p_s, k_s, v_s, _, _, gt_s = _inproj(xs, w_a, w_b, w_gate, n_s * n_q)
    hm_s, c_s, nn_s, mm_s = _mlstm(
        p_s, gt_s, bg, wn, state_C[0], state_n[0][:, :, None, :], state_m[0][:, :, None, None],
        batch=n_s, L=n_q, out_dtype=F32)
    hs_s = _sb_sample(p_s, k_s, v_s, cache_k.reshape(-1, DH_SB), cache_v.reshape(-1, DH_SB),
                      page_table, bsb, n_q=n_q)
    x1_s, x1b_s = _outproj(hm_s, hs_s, xs, wo, g1, b1, n_s * n_q)
    s0u, s0g = _halves(state_conv[0][:, 0, :])
    s1u, s1g = _halves(state_conv[0][:, 1, :])
    act_s, hu_s, hg_s = _ffn_up_sample(x1b_s, w_up[0], wcu, wcg, bcu, bcg, s0u, s1u, s0g, s1g, n_q)
    y_s = _ffn_down(act_s, wd, x1_s, g2, b2, n_s * n_q)

    conv_s = _unsplit(hu_s.reshape(n_s, n_q, D_FF_PAD)[:, n_q - 2:], hg_s.reshape(n_s, n_q, D_FF_PAD)[:, n_q - 2:])
    return (
        y_p.reshape(1, seq, D_MODEL),
        y_s.reshape(n_s, n_q, D_MODEL),
        k_p.reshape(1, 1, seq, H_SB, DH_SB),
        v_p.reshape(1, 1, seq, H_SB, DH_SB),
        c_p[None],
        nn_p.reshape(1, 1, H_M, DK_M),
        mm_p.reshape(1, 1, H_M),
        _unsplit(cu_p, cg_p)[None, None],
        k_s.reshape(1, n_s, n_q, H_SB, DH_SB),
        v_s.reshape(1, n_s, n_q, H_SB, DH_SB),
        c_s[None],
        nn_s.reshape(1, n_s, H_M, DK_M),
        mm_s.reshape(1, n_s, H_M),
        conv_s[None],
    )
```
